```python
import math
import jax, jax.numpy as jnp
from jax import lax
import numpy as np

D_MODEL = 2048
BATCH = 8
SEQ = 2048
DEPTH = 1

N_META = 16
EPS = 1e-6
HEAD_DIM = 64
N_Q_HEADS = 16
N_KV_HEADS = 2
GROUP = N_Q_HEADS // N_KV_HEADS
WINDOW = 128
BLOCK = 128
Q_DIM = N_Q_HEADS * HEAD_DIM
KV_DIM = N_KV_HEADS * HEAD_DIM
LRU_WIDTH = 1024
LRU_BLOCKS = 16
LRU_BW = LRU_WIDTH // LRU_BLOCKS
CONV_W = 4
LRU_C = 8.0
IN_DIM = Q_DIM + 2 * KV_DIM + 2 * LRU_WIDTH
MIX_DIM = Q_DIM + LRU_WIDTH
PEER_HEADS = 8
N_KEYS = 128
N_EXPERTS = N_KEYS * N_KEYS
PEER_TOPK = 16
PEER_DK = 256
PEER_DK_HALF = PEER_DK // 2
PEER_CHUNK = 128

kernel_name = "hymba_swa_rglru_peer_block"


def rmsnorm(x, g):
    xf = x.astype(jnp.float32)
    y = xf * lax.rsqrt(jnp.mean(xf * xf, axis=-1, keepdims=True) + EPS)
    return (y * g.astype(jnp.float32)).astype(x.dtype)


def sink_softmax(s, mask, sink_b):
    s = jnp.where(mask, s, -1e30)
    sink = jnp.broadcast_to(sink_b, s.shape[:-1] + (1,))
    p = jax.nn.softmax(jnp.concatenate([s, sink], axis=-1), axis=-1)
    return p[..., :-1]


def window_attention(q, k, v, sinks):
    B, T, _ = q.shape
    S = T - N_META
    nb = S // BLOCK
    scale = HEAD_DIM ** -0.5
    q = q.reshape(B, T, N_KV_HEADS, GROUP, HEAD_DIM)
    k = k.reshape(B, T, N_KV_HEADS, HEAD_DIM)
    v = v.reshape(B, T, N_KV_HEADS, HEAD_DIM)
    sink_hg = sinks.astype(jnp.float32).reshape(N_KV_HEADS, GROUP)
    qm, qr = q[:, :N_META], q[:, N_META:]
    km, kr = k[:, :N_META], k[:, N_META:]
    vm, vr = v[:, :N_META], v[:, N_META:]

    s_m = jnp.einsum('bqhgd,bkhd->bhgqk', qm, km).astype(jnp.float32) * scale
    mask_m = jnp.tril(jnp.ones((N_META, N_META), bool))
    p_m = sink_softmax(s_m, mask_m, sink_hg[None, :, :, None, None])
    o_m = jnp.einsum('bhgqk,bkhd->bqhgd', p_m.astype(v.dtype), vm).reshape(B, N_META, Q_DIM)

    qb = qr.reshape(B, nb, BLOCK, N_KV_HEADS, GROUP, HEAD_DIM)
    kp = jnp.pad(kr, ((0, 0), (BLOCK, 0), (0, 0), (0, 0))).reshape(B, nb + 1, BLOCK, N_KV_HEADS, HEAD_DIM)
    vp = jnp.pad(vr, ((0, 0), (BLOCK, 0), (0, 0), (0, 0))).reshape(B, nb + 1, BLOCK, N_KV_HEADS, HEAD_DIM)
    k_meta = jnp.broadcast_to(km[:, None], (B, nb, N_META, N_KV_HEADS, HEAD_DIM))
    v_meta = jnp.broadcast_to(vm[:, None], (B, nb, N_META, N_KV_HEADS, HEAD_DIM))
    k_all = jnp.concatenate([k_meta, kp[:, :-1], kp[:, 1:]], axis=2)
    v_all = jnp.concatenate([v_meta, vp[:, :-1], vp[:, 1:]], axis=2)
    s_r = jnp.einsum('bnqhgd,bnkhd->bnhgqk', qb, k_all).astype(jnp.float32) * scale
    q_coord = jnp.arange(BLOCK)[:, None] + BLOCK
    k_coord = jnp.arange(2 * BLOCK)[None, :]
    dist = q_coord - k_coord
    band = (dist >= 0) & (dist < WINDOW)
    valid = (jnp.arange(nb)[:, None] > 0) | (k_coord >= BLOCK)
    mask_r = jnp.concatenate([jnp.ones((nb, BLOCK, N_META), bool),
                              band[None] & valid[:, None, :]], axis=-1)
    p_r = sink_softmax(s_r, mask_r[None, :, None, None], sink_hg[None, None, :, :, None, None])
    o_r = jnp.einsum('bnhgqk,bnkhd->bnqhgd', p_r.astype(v.dtype), v_all).reshape(B, S, Q_DIM)
    return jnp.concatenate([o_m, o_r], axis=1)


def rglru_branch(xb, gate, conv_w, conv_b, w_r, b_r, w_i, b_i, lam):
    B, T, W = xb.shape
    xp = jnp.pad(xb, ((0, 0), (CONV_W - 1, 0), (0, 0)))
    xc = conv_b
    for tap in range(CONV_W):
        xc = xc + conv_w[tap] * xp[:, tap:tap + T]
    xh = xc.reshape(B, T, LRU_BLOCKS, LRU_BW)
    r = jax.nn.sigmoid(jnp.einsum('btnc,ncd->btnd', xh, w_r) + b_r).reshape(B, T, W)
    i = jax.nn.sigmoid(jnp.einsum('btnc,ncd->btnd', xh, w_i) + b_i).reshape(B, T, W)
    log_a = LRU_C * r.astype(jnp.float32) * jax.nn.log_sigmoid(lam.astype(jnp.float32))
    a = jnp.exp(log_a)
    mult = jnp.sqrt(-jnp.expm1(2.0 * log_a))
    bx = mult * (i * xc).astype(jnp.float32)

    def combine(left, right):
        a_l, b_l = left
        a_r, b_r_ = right
        return a_l * a_r, a_r * b_l + b_r_

    _, h = lax.associative_scan(combine, (a, bx), axis=1)
    return (jax.nn.gelu(gate) * h.astype(xb.dtype))


def peer(x2d, w_q, sub_keys, u_tab, v_tab):
    n_tok, D = x2d.shape
    pad = (-n_tok) % PEER_CHUNK
    xp = jnp.pad(x2d, ((0, pad), (0, 0))).reshape(-1, PEER_CHUNK, D)

    def one_block(xc):
        q = (xc @ w_q).reshape(PEER_CHUNK, PEER_HEADS, 2, PEER_DK_HALF)
        s = jnp.einsum('chpd,hpkd->chpk', q, sub_keys).astype(jnp.float32)
        top_s, top_i = lax.top_k(s, PEER_TOPK)
        cand = top_s[:, :, 0, :, None] + top_s[:, :, 1, None, :]
        best_s, best_c = lax.top_k(cand.reshape(PEER_CHUNK, PEER_HEADS, PEER_TOPK * PEER_TOPK), PEER_TOPK)
        i1 = jnp.take_along_axis(top_i[:, :, 0], best_c // PEER_TOPK, axis=-1)
        i2 = jnp.take_along_axis(top_i[:, :, 1], best_c % PEER_TOPK, axis=-1)
        experts = i1 * N_KEYS + i2
        g = jax.nn.softmax(best_s, axis=-1)
        u = u_tab[experts]
        act = jax.nn.gelu(jnp.einsum('chkd,cd->chk', u, xc))
        v = v_tab[experts]
        return jnp.einsum('chk,chkd->cd', (g * act.astype(jnp.float32)).astype(v.dtype), v)

    out = lax.map(one_block, xp)
    return out.reshape(-1, D)[:n_tok]


def setup_inputs(seed: int = 0) -> dict:
    key = jax.random.key(seed)
    ks = jax.random.split(key, 24)
    f32 = jnp.float32

    def nrm(k, shape, scale):
        return jax.random.normal(k, shape, f32) * scale

    def gain(k, shape):
        return 1.0 + 0.02 * jax.random.normal(k, shape, f32)

    u_a = jax.random.uniform(ks[10], (DEPTH, LRU_WIDTH), f32, 0.9, 0.999)
    sig_l = u_a ** (1.0 / LRU_C)
    lru_lambda = jnp.log(sig_l) - jnp.log1p(-sig_l)
    return {
        "x": jax.random.normal(ks[0], (BATCH, SEQ, D_MODEL), f32),
        "meta_tokens": nrm(ks[1], (N_META, D_MODEL), 1.0),
        "norm_mix_g": gain(ks[2], (DEPTH, D_MODEL)),
        "w_in": nrm(ks[3], (DEPTH, D_MODEL, IN_DIM), D_MODEL ** -0.5),
        "b_in": nrm(ks[4], (DEPTH, IN_DIM), 0.02),
        "sinks": nrm(ks[5], (DEPTH, N_Q_HEADS), 0.5),
        "conv_w": nrm(ks[6], (DEPTH, CONV_W, LRU_WIDTH), CONV_W ** -0.5),
        "conv_b": nrm(ks[7], (DEPTH, LRU_WIDTH), 0.02),
        "w_r": nrm(ks[8], (DEPTH, LRU_BLOCKS, LRU_BW, LRU_BW), LRU_BW ** -0.5),
        "b_r": nrm(ks[9], (DEPTH, LRU_BLOCKS, LRU_BW), 0.02),
        "w_i": nrm(ks[11], (DEPTH, LRU_BLOCKS, LRU_BW, LRU_BW), LRU_BW ** -0.5),
        "b_i": nrm(ks[12], (DEPTH, LRU_BLOCKS, LRU_BW), 0.02),
        "lru_lambda": lru_lambda,
        "gn_attn_g": gain(ks[13], (DEPTH, Q_DIM)),
        "gn_lru_g": gain(ks[14], (DEPTH, LRU_WIDTH)),
        "w_out": nrm(ks[15], (DEPTH, MIX_DIM, D_MODEL), MIX_DIM ** -0.5),
        "norm_ffn_g": gain(ks[16], (DEPTH, D_MODEL)),
        "peer_wq": nrm(ks[17], (DEPTH, D_MODEL, PEER_HEADS * PEER_DK), D_MODEL ** -0.5),
        "peer_sub_keys": nrm(ks[18], (DEPTH, PEER_HEADS, 2, N_KEYS, PEER_DK_HALF), PEER_DK_HALF ** -0.5),
        "peer_u": nrm(ks[19], (DEPTH, N_EXPERTS, D_MODEL), D_MODEL ** -0.5),
        "peer_v": nrm(ks[20], (DEPTH, N_EXPERTS, D_MODEL), 0.5),
        "final_norm_g": gain(ks[21], (D_MODEL,)),
    }


def reference(x, meta_tokens, norm_mix_g, w_in, b_in, sinks, conv_w, conv_b, w_r, b_r, w_i, b_i,
              lru_lambda, gn_attn_g, gn_lru_g, w_out, norm_ffn_g, peer_wq, peer_sub_keys,
              peer_u, peer_v, final_norm_g):
    B = x.shape[0]
    meta = jnp.broadcast_to(meta_tokens[None].astype(x.dtype), (B, N_META, D_MODEL))
    h = jnp.concatenate([meta, x], axis=1)
    T = h.shape[1]
    split_at = [Q_DIM, Q_DIM + KV_DIM, Q_DIM + 2 * KV_DIM, Q_DIM + 2 * KV_DIM + LRU_WIDTH]
    for l in range(DEPTH):
        hn = rmsnorm(h, norm_mix_g[l])
        proj = hn @ w_in[l] + b_in[l]
        q, k, v, xb, gate = jnp.split(proj, split_at, axis=-1)
        o_attn = window_attention(q, k, v, sinks[l])
        o_lru = rglru_branch(xb, gate, conv_w[l], conv_b[l], w_r[l], b_r[l], w_i[l], b_i[l],
                             lru_lambda[l])
        mixed = jnp.concatenate([rmsnorm(o_attn, gn_attn_g[l]), rmsnorm(o_lru, gn_lru_g[l])], axis=-1)
        h = h + mixed @ w_out[l]
        hn = rmsnorm(h, norm_ffn_g[l]).reshape(B * T, D_MODEL)
        h = h + peer(hn, peer_wq[l], peer_sub_keys[l], peer_u[l], peer_v[l]).reshape(B, T, D_MODEL)
    y = rmsnorm(h, final_norm_g)
    return y[:, N_META:]
```

```python
import functools

import jax
import jax.numpy as jnp
from jax import lax
from jax.experimental import pallas as pl
from jax.experimental.pallas import tpu as pltpu

F32 = jnp.float32
BF16 = jnp.bfloat16
I32 = jnp.int32

EPS = 1e-6
N_META = 16
HEAD_DIM = 64
N_Q_HEADS = 16
N_KV_HEADS = 2
GROUP = N_Q_HEADS // N_KV_HEADS
Q_DIM = N_Q_HEADS * HEAD_DIM
KV_DIM = N_KV_HEADS * HEAD_DIM
ATT_BLOCK = 128
LRU_WIDTH = 1024
LRU_BLOCKS = 16
LRU_BW = LRU_WIDTH // LRU_BLOCKS
LRU_C = 8.0
GATE_GROUP = 256
PEER_HEADS = 8
N_KEYS = 128
PEER_TOPK = 16
PEER_ROWS = PEER_HEADS * PEER_TOPK
NEG_BIG = -1e30

VMEM_LIMIT = 56 * 1024 * 1024


def _cparams(sem, vmem=VMEM_LIMIT):
    return pltpu.CompilerParams(dimension_semantics=sem, vmem_limit_bytes=vmem)


def _const_spec(shape):
    nd = len(shape)
    return pl.BlockSpec(shape, lambda *_: (0,) * nd, pipeline_mode=pl.Buffered(1))


def _rms(x, g):
    return x * lax.rsqrt(jnp.mean(x * x, axis=-1, keepdims=True) + EPS) * g


def _gelu(x):
    return jax.nn.gelu(x, approximate=True)


def _proj_kernel(x_ref, g_ref, w_ref, b_ref, qkv_ref, xb_ref, gate_ref):
    hn = _rms(x_ref[...], g_ref[...]).astype(BF16)
    n_qkv = Q_DIM + 2 * KV_DIM

    def proj(lo, hi):
        return jnp.dot(hn, w_ref[:, lo:hi], preferred_element_type=F32) + b_ref[:, lo:hi]

    step = 256
    for lo in range(0, n_qkv, step):
        qkv_ref[:, lo:lo + step] = proj(lo, lo + step).astype(BF16)
    for lo in range(0, LRU_WIDTH, step):
        xb_ref[:, lo:lo + step] = proj(n_qkv + lo, n_qkv + lo + step)
        gate_ref[:, lo:lo + step] = proj(n_qkv + LRU_WIDTH + lo, n_qkv + LRU_WIDTH + lo + step)


def _input_proj(x2, g, w_bf, b, tm):
    n, d = x2.shape
    n_qkv = Q_DIM + 2 * KV_DIM
    in_dim = w_bf.shape[1]
    return pl.pallas_call(
        _proj_kernel,
        grid=(n // tm,),
        in_specs=[
            pl.BlockSpec((tm, d), lambda i: (i, 0)),
            _const_spec((1, d)),
            _const_spec((d, in_dim)),
            _const_spec((1, in_dim)),
        ],
        out_specs=[
            pl.BlockSpec((tm, n_qkv), lambda i: (i, 0)),
            pl.BlockSpec((tm, LRU_WIDTH), lambda i: (i, 0)),
            pl.BlockSpec((tm, LRU_WIDTH), lambda i: (i, 0)),
        ],
        out_shape=[
            jax.ShapeDtypeStruct((n, n_qkv), BF16),
            jax.ShapeDtypeStruct((n, LRU_WIDTH), F32),
            jax.ShapeDtypeStruct((n, LRU_WIDTH), F32),
        ],
        compiler_params=_cparams(("parallel",)),
        name="input_proj",
    )(x2, g, w_bf, b)


def _attn_kernel(nb, q_ref, kc_ref, kp_ref, vc_ref, vp_ref, km_ref, vm_ref, sink_ref, g_ref,
                 o_ref, acc_ref):
    blk = pl.program_id(0) % nb
    row = lax.broadcasted_iota(I32, (ATT_BLOCK, 2 * ATT_BLOCK), 0)
    col = lax.broadcasted_iota(I32, (ATT_BLOCK, 2 * ATT_BLOCK), 1)
    no_prev = jnp.where(blk > 0, 0, 2 * ATT_BLOCK)
    in_prev = jnp.logical_and(col < ATT_BLOCK, col > row + no_prev)
    in_cur = jnp.logical_and(col >= ATT_BLOCK, (col - ATT_BLOCK) <= row)
    band = jnp.logical_or(in_prev, in_cur)
    scale = HEAD_DIM ** -0.5
    nt = (((1,), (1,)), ((), ()))
    for g in range(N_KV_HEADS):
        ks = slice(g * HEAD_DIM, (g + 1) * HEAD_DIM)
        kb = jnp.concatenate([kp_ref[:, ks], kc_ref[:, ks]], axis=0)
        vb = jnp.concatenate([vp_ref[:, ks], vc_ref[:, ks]], axis=0)
        kmg = km_ref[:, ks]
        vmg = vm_ref[:, ks]
        for hh in range(GROUP):
            h = g * GROUP + hh
            qh = q_ref[:, h * HEAD_DIM:(h + 1) * HEAD_DIM]
            sb = lax.dot_general(qh, kb, nt, preferred_element_type=F32) * scale
            sm = lax.dot_general(qh, kmg, nt, preferred_element_type=F32) * scale
            sb = jnp.where(band, sb, NEG_BIG)
            sink = sink_ref[h]
            m = jnp.maximum(jnp.max(sb, axis=-1, keepdims=True), jnp.max(sm, axis=-1, keepdims=True))
            m = jnp.maximum(m, sink)
            pb = jnp.exp(sb - m)
            pm = jnp.exp(sm - m)
            den = (jnp.sum(pb, axis=-1, keepdims=True) + jnp.sum(pm, axis=-1, keepdims=True)
                   + jnp.exp(sink - m))
            o = (jnp.dot(pb.astype(BF16), vb, preferred_element_type=F32)
                 + jnp.dot(pm.astype(BF16), vmg, preferred_element_type=F32))
            acc_ref[:, h * HEAD_DIM:(h + 1) * HEAD_DIM] = o / den
    o_ref[...] = _rms(acc_ref[...], g_ref[...]).astype(BF16)


def _attention(qkv, km, vm, sinks, g, n_batch):
    n = qkv.shape[0]
    nblk = n // ATT_BLOCK
    nb = nblk // n_batch
    kcol = Q_DIM // KV_DIM
    return pl.pallas_call(
        functools.partial(_attn_kernel, nb),
        grid=(nblk,),
        in_specs=[
            pl.BlockSpec((ATT_BLOCK, Q_DIM), lambda i: (i, 0)),
            pl.BlockSpec((ATT_BLOCK, KV_DIM), lambda i: (i, kcol)),
            pl.BlockSpec((ATT_BLOCK, KV_DIM), lambda i: (jnp.maximum(i - 1, 0), kcol)),
            pl.BlockSpec((ATT_BLOCK, KV_DIM), lambda i: (i, kcol + 1)),
            pl.BlockSpec((ATT_BLOCK, KV_DIM), lambda i: (jnp.maximum(i - 1, 0), kcol + 1)),
            _const_spec((N_META, KV_DIM)),
            _const_spec((N_META, KV_DIM)),
            pl.BlockSpec(memory_space=pltpu.SMEM),
            _const_spec((1, Q_DIM)),
        ],
        out_specs=pl.BlockSpec((ATT_BLOCK, Q_DIM), lambda i: (i, 0)),
        out_shape=jax.ShapeDtypeStruct((n, Q_DIM), BF16),
        scratch_shapes=[pltpu.VMEM((ATT_BLOCK, Q_DIM), F32)],
        compiler_params=_cparams(("parallel",)),
        name="swa_attention",
    )(qkv, qkv, qkv, qkv, qkv, km, vm, sinks, g)


def _log_sigmoid(x):
    return -(jnp.maximum(-x, 0.0) + jnp.log1p(jnp.exp(-jnp.abs(x))))


def _lru_kernel(tt, x_ref, gate_ref, cw_ref, cb_ref, wr_ref, br_ref, wi_ref, bi_ref, lam_ref, g_ref,
                h0_ref, tail0_ref, o_ref, hlast_ref, xfull_ref, h_ref):
    t = pl.program_id(1)

    @pl.when(t == 0)
    def _():
        h_ref[...] = h0_ref[...]
        xfull_ref[0:8, :] = tail0_ref[...]

    x = x_ref[...]
    xfull_ref[8:8 + tt, :] = x
    xc = (cb_ref[...]
          + cw_ref[0:1, :] * xfull_ref[5:5 + tt, :]
          + cw_ref[1:2, :] * xfull_ref[6:6 + tt, :]
          + cw_ref[2:3, :] * xfull_ref[7:7 + tt, :]
          + cw_ref[3:4, :] * x)
    xfull_ref[0:8, :] = x[tt - 8:tt, :]

    xcb = xc.astype(BF16)
    r_parts, i_parts = [], []
    for gi in range(LRU_WIDTH // GATE_GROUP):
        sl = slice(gi * GATE_GROUP, (gi + 1) * GATE_GROUP)
        r_parts.append(jnp.dot(xcb[:, sl], wr_ref[gi], preferred_element_type=F32))
        i_parts.append(jnp.dot(xcb[:, sl], wi_ref[gi], preferred_element_type=F32))
    r = jax.nn.sigmoid(jnp.concatenate(r_parts, axis=-1) + br_ref[...])
    ig = jax.nn.sigmoid(jnp.concatenate(i_parts, axis=-1) + bi_ref[...])
    log_a = LRU_C * r * _log_sigmoid(lam_ref[...])
    a = jnp.exp(log_a)
    th = jnp.tanh(log_a)
    b = jnp.sqrt(-2.0 * th / (1.0 - th)) * (ig * xc)

    rows = lax.broadcasted_iota(I32, a.shape, 0)
    s = 1
    while s < tt:
        live = rows >= s
        a_sh = pltpu.roll(a, s, 0)
        b_sh = pltpu.roll(b, s, 0)
        b = jnp.where(live, a * b_sh + b, b)
        a = jnp.where(live, a * a_sh, a)
        s *= 2
    h = b + a * h_ref[0:1, :]
    hl = h[tt - 1:tt, :]
    h_ref[...] = jnp.broadcast_to(hl, h_ref.shape)
    hlast_ref[...] = jnp.broadcast_to(hl, hlast_ref.shape)

    o_ref[...] = _rms(_gelu(gate_ref[...]) * h, g_ref[...]).astype(BF16)


def _lru(xb, gate, cw, cb, wr_bd, br, wi_bd, bi, lam, g, h0, tail0, n_batch, tt):
    n = xb.shape[0]
    nt = n // n_batch // tt
    w = LRU_WIDTH
    ng = w // GATE_GROUP
    row_spec = pl.BlockSpec((tt, w), lambda b, t: (b * nt + t, 0))
    return pl.pallas_call(
        functools.partial(_lru_kernel, tt),
        grid=(n_batch, nt),
        in_specs=[
            row_spec, row_spec,
            _const_spec((4, w)), _const_spec((1, w)),
            _const_spec((ng, GATE_GROUP, GATE_GROUP)), _const_spec((1, w)),
            _const_spec((ng, GATE_GROUP, GATE_GROUP)), _const_spec((1, w)),
            _const_spec((1, w)), _const_spec((1, w)),
            _const_spec((8, w)), _const_spec((8, w)),
        ],
        out_specs=[row_spec, pl.BlockSpec((8, w), lambda b, t: (b * nt + t, 0))],
        out_shape=[jax.ShapeDtypeStruct((n, w), BF16),
                   jax.ShapeDtypeStruct((n_batch * nt * 8, w), F32)],
        scratch_shapes=[pltpu.VMEM((tt + 8, w), F32), pltpu.VMEM((8, w), F32)],
        compiler_params=_cparams(("arbitrary", "arbitrary")),
        name="rg_lru",
    )(xb, gate, cw, cb, wr_bd, br, wi_bd, bi, lam, g, h0, tail0)


def _block_diag_groups(w):
    per = GATE_GROUP // LRU_BW
    w4 = w.reshape(LRU_BLOCKS // per, per, LRU_BW, LRU_BW)
    eye = jnp.eye(per, dtype=w.dtype)
    return jnp.einsum('gacd,ab->gacbd', w4, eye).reshape(LRU_BLOCKS // per, GATE_GROUP, GATE_GROUP)


def _pack_bf16_pair(lo, hi):
    lo_bits = lax.bitcast_convert_type(lo.astype(BF16).astype(F32), I32)
    hi_bits = lax.bitcast_convert_type(hi.astype(BF16).astype(F32), I32)
    return jnp.bitwise_or(lax.shift_right_logical(lo_bits, 16), jnp.bitwise_and(hi_bits, -65536))


def _unpack_bf16_pair(w):
    lo = lax.bitcast_convert_type(lax.shift_left(w, 16), F32)
    hi = lax.bitcast_convert_type(jnp.bitwise_and(w, -65536), F32)
    return lo, hi


def _mix_kernel(a_ref, l_ref, x_ref, wa_ref, wl_ref, g_ref, wq_ref, h_ref, xw_ref, q_ref):
    h = (x_ref[...]
         + jnp.dot(a_ref[...], wa_ref[...], preferred_element_type=F32)
         + jnp.dot(l_ref[...], wl_ref[...], preferred_element_type=F32))
    h_ref[...] = h
    hn = _rms(h, g_ref[...])
    half = hn.shape[1] // 2
    xw_ref[...] = _pack_bf16_pair(hn[:, :half], hn[:, half:])
    q_ref[...] = jnp.dot(hn.astype(BF16), wq_ref[...], preferred_element_type=F32).astype(BF16)


def _mix(attn_n, lru_n, x2, wa, wl, g, wq, tm):
    n, d = x2.shape
    dq = wq.shape[1]
    return pl.pallas_call(
        _mix_kernel,
        grid=(n // tm,),
        in_specs=[
            pl.BlockSpec((tm, Q_DIM), lambda i: (i, 0)),
            pl.BlockSpec((tm, LRU_WIDTH), lambda i: (i, 0)),
            pl.BlockSpec((tm, d), lambda i: (i, 0)),
            _const_spec((Q_DIM, d)), _const_spec((LRU_WIDTH, d)), _const_spec((1, d)),
            _const_spec((d, dq)),
        ],
        out_specs=[
            pl.BlockSpec((tm, d), lambda i: (i, 0)),
            pl.BlockSpec((tm, d // 2), lambda i: (i, 0)),
            pl.BlockSpec((tm, dq), lambda i: (i, 0)),
        ],
        out_shape=[
            jax.ShapeDtypeStruct((n, d), F32),
            jax.ShapeDtypeStruct((n, d // 2), I32),
            jax.ShapeDtypeStruct((n, dq), BF16),
        ],
        compiler_params=_cparams(("parallel",)),
        name="out_proj_peer_query",
    )(attn_n, lru_n, x2, wa, wl, g, wq)


def _top16_rows(s, ids):
    big = jnp.int32(2 ** 30)
    vals, picks = [], []
    for _ in range(PEER_TOPK):
        m = jnp.max(s, axis=0, keepdims=True)
        pick = jnp.min(jnp.where(s == m, ids, big), axis=0, keepdims=True)
        s = jnp.where(ids == pick, -jnp.inf, s)
        vals.append(m)
        picks.append(pick)
    return jnp.concatenate(vals, axis=0), jnp.concatenate(picks, axis=0)


def _retrieve_kernel(q_ref, keys_ref, e_ref, g_ref):
    tm = q_ref.shape[0]
    nt = (((1,), (1,)), ((), ()))
    key_ids = lax.broadcasted_iota(I32, (N_KEYS, tm), 0)
    tops = []
    for p in range(2):
        qp = q_ref[:, p * N_KEYS:(p + 1) * N_KEYS]
        s = lax.dot_general(keys_ref[0, p], qp, nt, preferred_element_type=F32)
        tops.append(_top16_rows(s, key_ids))
    (v0, i0), (v1, i1) = tops
    cand = [v0[0:1] + v1]
    cid = [lax.broadcasted_iota(I32, (PEER_TOPK, tm), 0)]
    eid = [i0[0:1] * N_KEYS + i1]
    for k1 in range(1, PEER_TOPK):
        cand.append(v0[k1:k1 + 1] + v1[0:8])
        cid.append(lax.broadcasted_iota(I32, (8, tm), 0) + k1 * PEER_TOPK)
        eid.append(i0[k1:k1 + 1] * N_KEYS + i1[0:8])
    cand = jnp.concatenate(cand, axis=0)
    cid = jnp.concatenate(cid, axis=0)
    eid = jnp.concatenate(eid, axis=0)
    best, experts = [], []
    big = jnp.int32(2 ** 30)
    for _ in range(PEER_TOPK):
        m = jnp.max(cand, axis=0, keepdims=True)
        pick = jnp.min(jnp.where(cand == m, cid, big), axis=0, keepdims=True)
        sel = cid == pick
        experts.append(jnp.max(jnp.where(sel, eid, -1), axis=0, keepdims=True))
        cand = jnp.where(sel, -jnp.inf, cand)
        best.append(m)
    best = jnp.concatenate(best, axis=0)
    ex = jnp.exp(best - best[0:1])
    g_ref[...] = ex / jnp.sum(ex, axis=0, keepdims=True)
    e_ref[...] = jnp.concatenate(experts, axis=0)


def _retrieve(q, keys_bf, tm):
    n = q.shape[0]
    dk = 2 * N_KEYS
    return pl.pallas_call(
        _retrieve_kernel,
        grid=(n // tm, PEER_HEADS),
        in_specs=[
            pl.BlockSpec((tm, dk), lambda i, h: (i, h)),
            pl.BlockSpec((1, 2, N_KEYS, N_KEYS), lambda i, h: (h, 0, 0, 0)),
        ],
        out_specs=[
            pl.BlockSpec((PEER_TOPK, tm), lambda i, h: (h, i)),
            pl.BlockSpec((PEER_TOPK, tm), lambda i, h: (h, i)),
        ],
        out_shape=[
            jax.ShapeDtypeStruct((PEER_ROWS, n), I32),
            jax.ShapeDtypeStruct((PEER_ROWS, n), F32),
        ],
        compiler_params=_cparams(("parallel", "arbitrary")),
        name="peer_retrieve",
    )(q, keys_bf)


def _pack_tables_kernel(u_ref, v_ref, o_ref):
    half = u_ref.shape[1] // 2
    o_ref[:, :half] = _pack_bf16_pair(u_ref[:, :half], u_ref[:, half:])
    o_ref[:, half:] = _pack_bf16_pair(v_ref[:, :half], v_ref[:, half:])


def _pack_tables(u, v, tr):
    ne, d = u.shape
    return pl.pallas_call(
        _pack_tables_kernel,
        grid=(ne // tr,),
        in_specs=[pl.BlockSpec((tr, d), lambda i: (i, 0)), pl.BlockSpec((tr, d), lambda i: (i, 0))],
        out_specs=pl.BlockSpec((tr, d), lambda i: (i, 0)),
        out_shape=jax.ShapeDtypeStruct((ne, d), I32),
        compiler_params=_cparams(("parallel",)),
        name="peer_pack_tables",
    )(u, v)


def _expert_kernel(tb, lane_blk, idx_ref, idxn_ref, tab_ref, xw_ref, gw_ref, h_ref, gf_ref,
                   y_ref, buf_ref, sem_ref, peer_ref):
    i = pl.program_id(0)
    nsteps = pl.num_programs(0)
    rows = tb * PEER_ROWS
    half = xw_ref.shape[1]
    slot = i % 2

    def issue(ids_ref, dst_slot):
        def body(t, carry):
            for j in range(PEER_ROWS):
                e = ids_ref[t, j]
                pltpu.make_async_copy(tab_ref.at[pl.ds(e, 1), :],
                                      buf_ref.at[dst_slot, pl.ds(t * PEER_ROWS + j, 1), :],
                                      sem_ref.at[dst_slot]).start()
            return carry
        lax.fori_loop(0, tb, body, 0)

    @pl.when(i == 0)
    def _():
        issue(idx_ref, 0)

    @pl.when(i + 1 < nsteps)
    def _():
        issue(idxn_ref, 1 - slot)

    pltpu.make_async_copy(tab_ref.at[pl.ds(0, rows), :], buf_ref.at[slot], sem_ref.at[slot]).wait()

    lane = lax.broadcasted_iota(I32, (PEER_ROWS, lane_blk), 1)
    base = (i * tb) % lane_blk
    pre = jnp.zeros((PEER_ROWS, lane_blk), F32)
    for t in range(tb):
        xlo, xhi = _unpack_bf16_pair(xw_ref[t:t + 1, :])
        ulo, uhi = _unpack_bf16_pair(buf_ref[slot, t * PEER_ROWS:(t + 1) * PEER_ROWS, 0:half])
        rowsum = jnp.sum(ulo * xlo + uhi * xhi, axis=-1, keepdims=True)
        pre = jnp.where(lane == base + t, rowsum, pre)
    w = gw_ref[...] * _gelu(pre)
    for t in range(tb):
        wcol = jnp.sum(jnp.where(lane == base + t, w, 0.0), axis=-1, keepdims=True)
        vlo, vhi = _unpack_bf16_pair(buf_ref[slot, t * PEER_ROWS:(t + 1) * PEER_ROWS, half:2 * half])
        peer_ref[t:t + 1, 0:half] = jnp.sum(vlo * wcol, axis=0, keepdims=True)
        peer_ref[t:t + 1, half:2 * half] = jnp.sum(vhi * wcol, axis=0, keepdims=True)
    y_ref[...] = _rms(h_ref[...] + peer_ref[...], gf_ref[...])


def _experts(idx, tab, xw, gw, h1, gf, tb):
    n, d = h1.shape
    lane_blk = min(n, 128)
    nsteps = n // tb
    per_blk = lane_blk // tb
    return pl.pallas_call(
        functools.partial(_expert_kernel, tb, lane_blk),
        grid=(nsteps,),
        in_specs=[
            pl.BlockSpec((tb, PEER_ROWS), lambda i: (i, 0), memory_space=pltpu.SMEM),
            pl.BlockSpec((tb, PEER_ROWS), lambda i: (jnp.minimum(i + 1, nsteps - 1), 0),
                         memory_space=pltpu.SMEM),
            pl.BlockSpec(memory_space=pl.ANY),
            pl.BlockSpec((tb, d // 2), lambda i: (i, 0)),
            pl.BlockSpec((PEER_ROWS, lane_blk), lambda i: (0, i // per_blk)),
            pl.BlockSpec((tb, d), lambda i: (i, 0)),
            _const_spec((1, d)),
        ],
        out_specs=pl.BlockSpec((tb, d), lambda i: (i, 0)),
        out_shape=jax.ShapeDtypeStruct((n, d), F32),
        scratch_shapes=[
            pltpu.VMEM((2, tb * PEER_ROWS, d), I32),
            pltpu.SemaphoreType.DMA((2,)),
            pltpu.VMEM((tb, d), F32),
        ],
        compiler_params=_cparams(("arbitrary",)),
        name="peer_experts",
    )(idx, idx, tab, xw, gw, h1, gf)


def _row_tile(n, pref):
    t = pref
    while n % t:
        t //= 2
    return t


def kernel(x, meta_tokens, norm_mix_g, w_in, b_in, sinks, conv_w, conv_b, w_r, b_r, w_i, b_i, lru_lambda, gn_attn_g, gn_lru_g, w_out, norm_ffn_g, peer_wq, peer_sub_keys, peer_u, peer_v, final_norm_g):
    n_batch, seq, d = x.shape
    n = n_batch * seq
    assert w_in.shape[0] == 1 and seq % ATT_BLOCK == 0
    x2 = x.reshape(n, d)
    row = lambda a: a.reshape(1, -1)

    w_in_bf = w_in[0].astype(BF16)
    g_mix = row(norm_mix_g[0])
    b_in_r = row(b_in[0])
    qkv, xb, gate = _input_proj(x2, g_mix, w_in_bf, b_in_r, _row_tile(n, 256))
    qkv_m, xb_m, gate_m = _input_proj(meta_tokens.astype(F32), g_mix, w_in_bf, b_in_r, N_META)

    km = qkv_m[:, Q_DIM:Q_DIM + KV_DIM]
    vm = qkv_m[:, Q_DIM + KV_DIM:]
    attn_n = _attention(qkv, km, vm, sinks[0].astype(F32), row(gn_attn_g[0]), n_batch)

    lru_args = (conv_w[0], row(conv_b[0]), _block_diag_groups(w_r[0]).astype(BF16), row(b_r[0]),
                _block_diag_groups(w_i[0]).astype(BF16), row(b_i[0]), row(lru_lambda[0]),
                row(gn_lru_g[0]))
    zeros8 = jnp.zeros((8, LRU_WIDTH), F32)
    _, h_meta = _lru(xb_m, gate_m, *lru_args, zeros8, zeros8, 1, N_META)
    lru_n, _ = _lru(xb, gate, *lru_args, h_meta, xb_m[N_META - 8:], n_batch, _row_tile(seq, 256))

    w_out_bf = w_out[0].astype(BF16)
    h1, xw, q = _mix(attn_n, lru_n, x2, w_out_bf[:Q_DIM], w_out_bf[Q_DIM:], row(norm_ffn_g[0]),
                     peer_wq[0].astype(BF16), _row_tile(n, 256))

    experts, gw = _retrieve(q, peer_sub_keys[0].astype(BF16), _row_tile(n, 256))
    tab = _pack_tables(peer_u[0], peer_v[0], 256)
    y = _experts(experts.T, tab, xw, gw, h1, row(final_norm_g), 8)
    return y.reshape(n_batch, seq, d)
```

```python
import functools

import numpy as np
import jax
import jax.numpy as jnp
from jax import lax
from jax.experimental import pallas as pl
from jax.experimental.pallas import tpu as pltpu

F32 = jnp.float32
BF16 = jnp.bfloat16
I32 = jnp.int32

EPS = 1e-6
N_META = 16
HEAD_DIM = 64
N_Q_HEADS = 16
N_KV_HEADS = 2
GROUP = N_Q_HEADS // N_KV_HEADS
Q_DIM = N_Q_HEADS * HEAD_DIM
KV_DIM = N_KV_HEADS * HEAD_DIM
ATT_BLOCK = 128
LRU_WIDTH = 1024
LRU_BLOCKS = 16
LRU_BW = LRU_WIDTH // LRU_BLOCKS
LRU_C = 8.0
GATE_GROUP = 256
PEER_HEADS = 8
N_KEYS = 128
PEER_TOPK = 16
PEER_ROWS = PEER_HEADS * PEER_TOPK
NEG_BIG = -1e30

VMEM_LIMIT = 56 * 1024 * 1024


def _cparams(sem, vmem=VMEM_LIMIT):
    return pltpu.CompilerParams(dimension_semantics=sem, vmem_limit_bytes=vmem)


def _const_spec(shape):
    nd = len(shape)
    return pl.BlockSpec(shape, lambda *_: (0,) * nd, pipeline_mode=pl.Buffered(1))


def _rms(x, g):
    return x * lax.rsqrt(jnp.mean(x * x, axis=-1, keepdims=True) + EPS) * g


def _gelu(x):
    return jax.nn.gelu(x, approximate=True)


def _proj_kernel(x_ref, g_ref, w_ref, b_ref, qkv_ref, xb_ref, gate_ref):
    hn = _rms(x_ref[...], g_ref[...]).astype(BF16)
    n_qkv = Q_DIM + 2 * KV_DIM

    def proj(lo, hi):
        return jnp.dot(hn, w_ref[:, lo:hi], preferred_element_type=F32) + b_ref[:, lo:hi]

    step = 256
    for lo in range(0, n_qkv, step):
        qkv_ref[:, lo:lo + step] = proj(lo, lo + step).astype(BF16)
    for lo in range(0, LRU_WIDTH, step):
        xb_ref[:, lo:lo + step] = proj(n_qkv + lo, n_qkv + lo + step)
        gate_ref[:, lo:lo + step] = proj(n_qkv + LRU_WIDTH + lo, n_qkv + LRU_WIDTH + lo + step)


def _input_proj(x2, g, w_bf, b, tm):
    n, d = x2.shape
    n_qkv = Q_DIM + 2 * KV_DIM
    in_dim = w_bf.shape[1]
    return pl.pallas_call(
        _proj_kernel,
        grid=(n // tm,),
        in_specs=[
            pl.BlockSpec((tm, d), lambda i: (i, 0)),
            _const_spec((1, d)),
            _const_spec((d, in_dim)),
            _const_spec((1, in_dim)),
        ],
        out_specs=[
            pl.BlockSpec((tm, n_qkv), lambda i: (i, 0)),
            pl.BlockSpec((tm, LRU_WIDTH), lambda i: (i, 0)),
            pl.BlockSpec((tm, LRU_WIDTH), lambda i: (i, 0)),
        ],
        out_shape=[
            jax.ShapeDtypeStruct((n, n_qkv), BF16),
            jax.ShapeDtypeStruct((n, LRU_WIDTH), F32),
            jax.ShapeDtypeStruct((n, LRU_WIDTH), F32),
        ],
        compiler_params=_cparams(("parallel",)),
        name="input_proj",
    )(x2, g, w_bf, b)


def _attn_kernel(nb, q_ref, kc_ref, kp_ref, vc_ref, vp_ref, km_ref, vm_ref, sink_ref, g_ref,
                 o_ref, acc_ref):
    blk = pl.program_id(0) % nb
    row = lax.broadcasted_iota(I32, (ATT_BLOCK, 2 * ATT_BLOCK), 0)
    col = lax.broadcasted_iota(I32, (ATT_BLOCK, 2 * ATT_BLOCK), 1)
    no_prev = jnp.where(blk > 0, 0, 2 * ATT_BLOCK)
    in_prev = jnp.logical_and(col < ATT_BLOCK, col > row + no_prev)
    in_cur = jnp.logical_and(col >= ATT_BLOCK, (col - ATT_BLOCK) <= row)
    band = jnp.logical_or(in_prev, in_cur)
    scale = HEAD_DIM ** -0.5
    nt = (((1,), (1,)), ((), ()))
    for g in range(N_KV_HEADS):
        ks = slice(g * HEAD_DIM, (g + 1) * HEAD_DIM)
        kb = jnp.concatenate([kp_ref[:, ks], kc_ref[:, ks]], axis=0)
        vb = jnp.concatenate([vp_ref[:, ks], vc_ref[:, ks]], axis=0)
        kmg = km_ref[:, ks]
        vmg = vm_ref[:, ks]
        for hh in range(GROUP):
            h = g * GROUP + hh
            qh = q_ref[:, h * HEAD_DIM:(h + 1) * HEAD_DIM]
            sb = lax.dot_general(qh, kb, nt, preferred_element_type=F32) * scale
            sm = lax.dot_general(qh, kmg, nt, preferred_element_type=F32) * scale
            sb = jnp.where(band, sb, NEG_BIG)
            sink = sink_ref[h]
            m = jnp.maximum(jnp.max(sb, axis=-1, keepdims=True), jnp.max(sm, axis=-1, keepdims=True))
            m = jnp.maximum(m, sink)
            pb = jnp.exp(sb - m)
            pm = jnp.exp(sm - m)
            den = (jnp.sum(pb, axis=-1, keepdims=True) + jnp.sum(pm, axis=-1, keepdims=True)
                   + jnp.exp(sink - m))
            o = (jnp.dot(pb.astype(BF16), vb, preferred_element_type=F32)
                 + jnp.dot(pm.astype(BF16), vmg, preferred_element_type=F32))
            acc_ref[:, h * HEAD_DIM:(h + 1) * HEAD_DIM] = o / den
    o_ref[...] = _rms(acc_ref[...], g_ref[...]).astype(BF16)


def _attention(qkv, km, vm, sinks, g, n_batch):
    n = qkv.shape[0]
    nblk = n // ATT_BLOCK
    nb = nblk // n_batch
    kcol = Q_DIM // KV_DIM
    return pl.pallas_call(
        functools.partial(_attn_kernel, nb),
        grid=(nblk,),
        in_specs=[
            pl.BlockSpec((ATT_BLOCK, Q_DIM), lambda i: (i, 0)),
            pl.BlockSpec((ATT_BLOCK, KV_DIM), lambda i: (i, kcol)),
            pl.BlockSpec((ATT_BLOCK, KV_DIM), lambda i: (jnp.maximum(i - 1, 0), kcol)),
            pl.BlockSpec((ATT_BLOCK, KV_DIM), lambda i: (i, kcol + 1)),
            pl.BlockSpec((ATT_BLOCK, KV_DIM), lambda i: (jnp.maximum(i - 1, 0), kcol + 1)),
            _const_spec((N_META, KV_DIM)),
            _const_spec((N_META, KV_DIM)),
            pl.BlockSpec(memory_space=pltpu.SMEM),
            _const_spec((1, Q_DIM)),
        ],
        out_specs=pl.BlockSpec((ATT_BLOCK, Q_DIM), lambda i: (i, 0)),
        out_shape=jax.ShapeDtypeStruct((n, Q_DIM), BF16),
        scratch_shapes=[pltpu.VMEM((ATT_BLOCK, Q_DIM), F32)],
        compiler_params=_cparams(("parallel",)),
        name="swa_attention",
    )(qkv, qkv, qkv, qkv, qkv, km, vm, sinks, g)


def _log_sigmoid(x):
    return -(jnp.maximum(-x, 0.0) + jnp.log1p(jnp.exp(-jnp.abs(x))))


def _lru_kernel(tt, x_ref, gate_ref, cw_ref, cb_ref, wr_ref, br_ref, wi_ref, bi_ref, lam_ref, g_ref,
                h0_ref, tail0_ref, o_ref, hlast_ref, xfull_ref, h_ref):
    t = pl.program_id(1)

    @pl.when(t == 0)
    def _():
        h_ref[...] = h0_ref[...]
        xfull_ref[0:8, :] = tail0_ref[...]

    x = x_ref[...]
    xfull_ref[8:8 + tt, :] = x
    xc = (cb_ref[...]
          + cw_ref[0:1, :] * xfull_ref[5:5 + tt, :]
          + cw_ref[1:2, :] * xfull_ref[6:6 + tt, :]
          + cw_ref[2:3, :] * xfull_ref[7:7 + tt, :]
          + cw_ref[3:4, :] * x)
    xfull_ref[0:8, :] = x[tt - 8:tt, :]

    xcb = xc.astype(BF16)
    r_parts, i_parts = [], []
    for gi in range(LRU_WIDTH // GATE_GROUP):
        sl = slice(gi * GATE_GROUP, (gi + 1) * GATE_GROUP)
        r_parts.append(jnp.dot(xcb[:, sl], wr_ref[gi], preferred_element_type=F32))
        i_parts.append(jnp.dot(xcb[:, sl], wi_ref[gi], preferred_element_type=F32))
    r = jax.nn.sigmoid(jnp.concatenate(r_parts, axis=-1) + br_ref[...])
    ig = jax.nn.sigmoid(jnp.concatenate(i_parts, axis=-1) + bi_ref[...])
    log_a = LRU_C * r * _log_sigmoid(lam_ref[...])
    a = jnp.exp(log_a)
    th = jnp.tanh(log_a)
    b = jnp.sqrt(-2.0 * th / (1.0 - th)) * (ig * xc)

    rows = lax.broadcasted_iota(I32, a.shape, 0)
    s = 1
    while s < tt:
        live = rows >= s
        a_sh = pltpu.roll(a, s, 0)
        b_sh = pltpu.roll(b, s, 0)
        b = jnp.where(live, a * b_sh + b, b)
        a = jnp.where(live, a * a_sh, a)
        s *= 2
    h = b + a * h_ref[0:1, :]
    hl = h[tt - 1:tt, :]
    h_ref[...] = jnp.broadcast_to(hl, h_ref.shape)
    hlast_ref[...] = jnp.broadcast_to(hl, hlast_ref.shape)

    o_ref[...] = _rms(_gelu(gate_ref[...]) * h, g_ref[...]).astype(BF16)


def _lru(xb, gate, cw, cb, wr_bd, br, wi_bd, bi, lam, g, h0, tail0, n_batch, tt):
    n = xb.shape[0]
    nt = n // n_batch // tt
    w = LRU_WIDTH
    ng = w // GATE_GROUP
    row_spec = pl.BlockSpec((tt, w), lambda b, t: (b * nt + t, 0))
    return pl.pallas_call(
        functools.partial(_lru_kernel, tt),
        grid=(n_batch, nt),
        in_specs=[
            row_spec, row_spec,
            _const_spec((4, w)), _const_spec((1, w)),
            _const_spec((ng, GATE_GROUP, GATE_GROUP)), _const_spec((1, w)),
            _const_spec((ng, GATE_GROUP, GATE_GROUP)), _const_spec((1, w)),
            _const_spec((1, w)), _const_spec((1, w)),
            _const_spec((8, w)), _const_spec((8, w)),
        ],
        out_specs=[row_spec, pl.BlockSpec((8, w), lambda b, t: (b * nt + t, 0))],
        out_shape=[jax.ShapeDtypeStruct((n, w), BF16),
                   jax.ShapeDtypeStruct((n_batch * nt * 8, w), F32)],
        scratch_shapes=[pltpu.VMEM((tt + 8, w), F32), pltpu.VMEM((8, w), F32)],
        compiler_params=_cparams(("arbitrary", "arbitrary")),
        name="rg_lru",
    )(xb, gate, cw, cb, wr_bd, br, wi_bd, bi, lam, g, h0, tail0)


def _block_diag_groups(w):
    per = GATE_GROUP // LRU_BW
    w4 = w.reshape(LRU_BLOCKS // per, per, LRU_BW, LRU_BW)
    eye = jnp.eye(per, dtype=w.dtype)
    return jnp.einsum('gacd,ab->gacbd', w4, eye).reshape(LRU_BLOCKS // per, GATE_GROUP, GATE_GROUP)


def _mix_kernel(a_ref, l_ref, x_ref, wa_ref, wl_ref, g_ref, wq_ref, h_ref, xw_ref, q_ref):
    h = (x_ref[...]
         + jnp.dot(a_ref[...], wa_ref[...], preferred_element_type=F32)
         + jnp.dot(l_ref[...], wl_ref[...], preferred_element_type=F32))
    h_ref[...] = h
    hn = _rms(h, g_ref[...]).astype(BF16)
    xw_ref[...] = hn
    q_ref[...] = jnp.dot(hn, wq_ref[...], preferred_element_type=F32).astype(BF16)


def _mix(attn_n, lru_n, x2, wa, wl, g, wq, tm):
    n, d = x2.shape
    dq = wq.shape[1]
    return pl.pallas_call(
        _mix_kernel,
        grid=(n // tm,),
        in_specs=[
            pl.BlockSpec((tm, Q_DIM), lambda i: (i, 0)),
            pl.BlockSpec((tm, LRU_WIDTH), lambda i: (i, 0)),
            pl.BlockSpec((tm, d), lambda i: (i, 0)),
            _const_spec((Q_DIM, d)), _const_spec((LRU_WIDTH, d)), _const_spec((1, d)),
            _const_spec((d, dq)),
        ],
        out_specs=[
            pl.BlockSpec((tm, d), lambda i: (i, 0)),
            pl.BlockSpec((tm, d), lambda i: (i, 0)),
            pl.BlockSpec((tm, dq), lambda i: (i, 0)),
        ],
        out_shape=[
            jax.ShapeDtypeStruct((n, d), F32),
            jax.ShapeDtypeStruct((n, d), BF16),
            jax.ShapeDtypeStruct((n, dq), BF16),
        ],
        compiler_params=_cparams(("parallel",)),
        name="out_proj_peer_query",
    )(attn_n, lru_n, x2, wa, wl, g, wq)


def _top16_rows(s, ids):
    big = jnp.int32(2 ** 30)
    vals, picks = [], []
    for _ in range(PEER_TOPK):
        m = jnp.max(s, axis=0, keepdims=True)
        pick = jnp.min(jnp.where(s == m, ids, big), axis=0, keepdims=True)
        s = jnp.where(ids == pick, -jnp.inf, s)
        vals.append(m)
        picks.append(pick)
    return jnp.concatenate(vals, axis=0), jnp.concatenate(picks, axis=0)


def _retrieve_kernel(q_ref, keys_ref, e_ref, g_ref):
    tm = q_ref.shape[0]
    nt = (((1,), (1,)), ((), ()))
    key_ids = lax.broadcasted_iota(I32, (N_KEYS, tm), 0)
    tops = []
    for p in range(2):
        qp = q_ref[:, p * N_KEYS:(p + 1) * N_KEYS]
        s = lax.dot_general(keys_ref[0, p], qp, nt, preferred_element_type=F32)
        tops.append(_top16_rows(s, key_ids))
    (v0, i0), (v1, i1) = tops
    cand = [v0[0:1] + v1]
    cid = [lax.broadcasted_iota(I32, (PEER_TOPK, tm), 0)]
    eid = [i0[0:1] * N_KEYS + i1]
    for k1 in range(1, PEER_TOPK):
        cand.append(v0[k1:k1 + 1] + v1[0:8])
        cid.append(lax.broadcasted_iota(I32, (8, tm), 0) + k1 * PEER_TOPK)
        eid.append(i0[k1:k1 + 1] * N_KEYS + i1[0:8])
    cand = jnp.concatenate(cand, axis=0)
    cid = jnp.concatenate(cid, axis=0)
    eid = jnp.concatenate(eid, axis=0)
    best, experts = [], []
    big = jnp.int32(2 ** 30)
    for _ in range(PEER_TOPK):
        m = jnp.max(cand, axis=0, keepdims=True)
        pick = jnp.min(jnp.where(cand == m, cid, big), axis=0, keepdims=True)
        sel = cid == pick
        experts.append(jnp.max(jnp.where(sel, eid, -1), axis=0, keepdims=True))
        cand = jnp.where(sel, -jnp.inf, cand)
        best.append(m)
    best = jnp.concatenate(best, axis=0)
    ex = jnp.exp(best - best[0:1])
    g_ref[...] = ex / jnp.sum(ex, axis=0, keepdims=True)
    e_ref[...] = jnp.concatenate(experts, axis=0)


def _retrieve(q, keys_bf, tm):
    n = q.shape[0]
    dk = 2 * N_KEYS
    return pl.pallas_call(
        _retrieve_kernel,
        grid=(n // tm, PEER_HEADS),
        in_specs=[
            pl.BlockSpec((tm, dk), lambda i, h: (i, h)),
            pl.BlockSpec((1, 2, N_KEYS, N_KEYS), lambda i, h: (h, 0, 0, 0)),
        ],
        out_specs=[
            pl.BlockSpec((PEER_TOPK, tm), lambda i, h: (h, i)),
            pl.BlockSpec((PEER_TOPK, tm), lambda i, h: (h, i)),
        ],
        out_shape=[
            jax.ShapeDtypeStruct((PEER_ROWS, n), I32),
            jax.ShapeDtypeStruct((PEER_ROWS, n), F32),
        ],
        compiler_params=_cparams(("parallel", "arbitrary")),
        name="peer_retrieve",
    )(q, keys_bf)


def _cast_tables_kernel(u_ref, v_ref, o_ref):
    d = u_ref.shape[1]
    o_ref[:, :d] = u_ref[...].astype(BF16)
    o_ref[:, d:] = v_ref[...].astype(BF16)


def _cast_tables(u, v, tr):
    ne, d = u.shape
    return pl.pallas_call(
        _cast_tables_kernel,
        grid=(ne // tr,),
        in_specs=[pl.BlockSpec((tr, d), lambda i: (i, 0)), pl.BlockSpec((tr, d), lambda i: (i, 0))],
        out_specs=pl.BlockSpec((tr, 2 * d), lambda i: (i, 0)),
        out_shape=jax.ShapeDtypeStruct((ne, 2 * d), BF16),
        compiler_params=_cparams(("parallel",)),
        name="peer_cast_tables",
    )(u, v)


SUBLANES = 8
LANES = 128


def _fold_pair(x, y, sh, rows):
    x2 = x + pltpu.roll(x, sh, 0)
    y2 = y + pltpu.roll(y, SUBLANES - sh, 0)
    return jnp.where((rows // sh) % 2 == 1, x2, y2)


def _fold_order():
    rows = np.arange(SUBLANES)[:, None]

    def fold(x, y, sh):
        x2 = x + np.roll(x, sh, 0)
        y2 = y + np.roll(y, SUBLANES - sh, 0)
        return np.where((rows // sh) % 2 == 1, x2, y2)

    vals = []
    for i in range(SUBLANES):
        v = np.zeros((SUBLANES, SUBLANES * SUBLANES), np.int64)
        v[np.arange(SUBLANES), i * SUBLANES + np.arange(SUBLANES)] = 1
        vals.append(v)
    sh = SUBLANES // 2
    while len(vals) > 1:
        vals = [fold(vals[2 * k], vals[2 * k + 1], sh) for k in range(len(vals) // 2)]
        sh //= 2
    out = vals[0].reshape(SUBLANES, SUBLANES, SUBLANES)
    src_of_row = [int(np.argmax(out[r].sum(-1))) for r in range(SUBLANES)]
    for r in range(SUBLANES):
        assert (out[r, src_of_row[r]] == 1).all() and out[r].sum() == SUBLANES
    order = [0] * SUBLANES
    for r, p in enumerate(src_of_row):
        order[p] = r
    return tuple(order)


_FOLD_ORDER = _fold_order()


def _sublane_sums(parts, rows):
    vals = [parts[r] for r in _FOLD_ORDER]
    sh = SUBLANES // 2
    while len(vals) > 1:
        vals = [_fold_pair(vals[2 * k], vals[2 * k + 1], sh, rows) for k in range(len(vals) // 2)]
        sh //= 2
    return vals[0]


def _expert_kernel(tb, lane_blk, idx_ref, idxn_ref, tab_ref, xw_ref, gw_ref, h_ref, gf_ref,
                   y_ref, buf_ref, sem_ref, wrep_ref):
    i = pl.program_id(0)
    nsteps = pl.num_programs(0)
    rows_per_slot = tb * PEER_ROWS
    ds = xw_ref.shape[1]
    slot = i % 2
    nslot = 1 - slot

    def row_copy(e, dst_slot, r):
        return pltpu.make_async_copy(tab_ref.at[e], buf_ref.at[dst_slot, r], sem_ref.at[dst_slot])

    @pl.when(i == 0)
    def _():
        def body(t, carry):
            for j in range(PEER_ROWS):
                row_copy(idx_ref[t, j], 0, t * PEER_ROWS + j).start()
            return carry
        lax.fori_loop(0, tb, body, 0)

    def slot_wait(s):
        pltpu.make_async_copy(tab_ref.at[pl.ds(0, rows_per_slot)], buf_ref.at[s], sem_ref.at[s]).wait()

    slot_wait(slot)

    rows8 = lax.broadcasted_iota(I32, (SUBLANES, LANES), 0)
    lane = lax.broadcasted_iota(I32, (PEER_ROWS, lane_blk), 1)
    base = (i * tb) % lane_blk
    pre = jnp.zeros((PEER_ROWS, lane_blk), F32)
    for t in range(tb):
        for j in range(PEER_ROWS):
            row_copy(idxn_ref[t, j], nslot, t * PEER_ROWS + j).start()
        x_t = xw_ref[t]
        sums = []
        for g0 in range(0, PEER_ROWS, SUBLANES):
            parts = []
            for j in range(g0, g0 + SUBLANES):
                p = (buf_ref[slot, t * PEER_ROWS + j, 0:ds, :] * x_t).astype(F32)
                parts.append(p[0:SUBLANES] + p[SUBLANES:2 * SUBLANES])
            sums.append(_sublane_sums(parts, rows8))
        rowsum = jnp.sum(jnp.concatenate(sums, axis=0), axis=-1, keepdims=True)
        pre = jnp.where(lane == base + t, rowsum, pre)
    w = gw_ref[...] * _gelu(pre)
    for t in range(tb):
        wcol = jnp.sum(jnp.where(lane == base + t, w, 0.0), axis=-1, keepdims=True)
        wrep_ref[t] = jnp.broadcast_to(wcol, (PEER_ROWS, LANES))
    n_acc = 4
    for t in range(tb):
        accs = [jnp.zeros((ds, LANES), F32) for _ in range(n_acc)]
        for j in range(PEER_ROWS):
            v = buf_ref[slot, t * PEER_ROWS + j, ds:2 * ds, :].astype(F32)
            accs[j % n_acc] = accs[j % n_acc] + v * wrep_ref[t, j:j + 1, :]
        hp = h_ref[t] + ((accs[0] + accs[1]) + (accs[2] + accs[3]))
        ms = jnp.mean(jnp.mean(hp * hp, axis=-1, keepdims=True), axis=0, keepdims=True)
        y_ref[t] = hp * lax.rsqrt(ms + EPS) * gf_ref[...]

    @pl.when(i == nsteps - 1)
    def _():
        slot_wait(nslot)


def _experts(idx, tab3, xw3, gw, h3, gf3, tb):
    n, ds, lanes = h3.shape
    lane_blk = min(n, LANES)
    nsteps = n // tb
    per_blk = lane_blk // tb
    return pl.pallas_call(
        functools.partial(_expert_kernel, tb, lane_blk),
        grid=(nsteps,),
        in_specs=[
            pl.BlockSpec((tb, PEER_ROWS), lambda i: (i, 0), memory_space=pltpu.SMEM),
            pl.BlockSpec((tb, PEER_ROWS), lambda i: (jnp.minimum(i + 1, nsteps - 1), 0),
                         memory_space=pltpu.SMEM),
            pl.BlockSpec(memory_space=pl.ANY),
            pl.BlockSpec((tb, ds, lanes), lambda i: (i, 0, 0)),
            pl.BlockSpec((PEER_ROWS, lane_blk), lambda i: (0, i // per_blk)),
            pl.BlockSpec((tb, ds, lanes), lambda i: (i, 0, 0)),
            _const_spec((ds, lanes)),
        ],
        out_specs=pl.BlockSpec((tb, ds, lanes), lambda i: (i, 0, 0)),
        out_shape=jax.ShapeDtypeStruct((n, ds, lanes), F32),
        scratch_shapes=[
            pltpu.VMEM((2, tb * PEER_ROWS, 2 * ds, lanes), BF16),
            pltpu.SemaphoreType.DMA((2,)),
            pltpu.VMEM((tb, PEER_ROWS, LANES), F32),
        ],
        compiler_params=_cparams(("arbitrary",)),
        name="peer_experts",
    )(idx, idx, tab3, xw3, gw, h3, gf3)


def _row_tile(n, pref):
    t = pref
    while n % t:
        t //= 2
    return t


def kernel(x, meta_tokens, norm_mix_g, w_in, b_in, sinks, conv_w, conv_b, w_r, b_r, w_i, b_i, lru_lambda, gn_attn_g, gn_lru_g, w_out, norm_ffn_g, peer_wq, peer_sub_keys, peer_u, peer_v, final_norm_g):
    n_batch, seq, d = x.shape
    n = n_batch * seq
    assert w_in.shape[0] == 1 and seq % ATT_BLOCK == 0
    x2 = x.reshape(n, d)
    row = lambda a: a.reshape(1, -1)

    w_in_bf = w_in[0].astype(BF16)
    g_mix = row(norm_mix_g[0])
    b_in_r = row(b_in[0])
    qkv, xb, gate = _input_proj(x2, g_mix, w_in_bf, b_in_r, _row_tile(n, 256))
    qkv_m, xb_m, gate_m = _input_proj(meta_tokens.astype(F32), g_mix, w_in_bf, b_in_r, N_META)

    km = qkv_m[:, Q_DIM:Q_DIM + KV_DIM]
    vm = qkv_m[:, Q_DIM + KV_DIM:]
    attn_n = _attention(qkv, km, vm, sinks[0].astype(F32), row(gn_attn_g[0]), n_batch)

    lru_args = (conv_w[0], row(conv_b[0]), _block_diag_groups(w_r[0]).astype(BF16), row(b_r[0]),
                _block_diag_groups(w_i[0]).astype(BF16), row(b_i[0]), row(lru_lambda[0]),
                row(gn_lru_g[0]))
    zeros8 = jnp.zeros((8, LRU_WIDTH), F32)
    _, h_meta = _lru(xb_m, gate_m, *lru_args, zeros8, zeros8, 1, N_META)
    lru_n, _ = _lru(xb, gate, *lru_args, h_meta, xb_m[N_META - 8:], n_batch, _row_tile(seq, 256))

    w_out_bf = w_out[0].astype(BF16)
    h1, xw, q = _mix(attn_n, lru_n, x2, w_out_bf[:Q_DIM], w_out_bf[Q_DIM:], row(norm_ffn_g[0]),
                     peer_wq[0].astype(BF16), _row_tile(n, 256))

    experts, gw = _retrieve(q, peer_sub_keys[0].astype(BF16), _row_tile(n, 256))
    ne = peer_u.shape[1]
    ds = d // LANES
    tab3 = _cast_tables(peer_u[0], peer_v[0], 256).reshape(ne, 2 * ds, LANES)
    y3 = _experts(experts.T, tab3, xw.reshape(n, ds, LANES), gw, h1.reshape(n, ds, LANES),
                  final_norm_g.reshape(ds, LANES), 8)
    return y3.reshape(n_batch, seq, d)
```

```python
import functools

import numpy as np
import jax
import jax.numpy as jnp
from jax import lax
from jax.experimental import pallas as pl
from jax.experimental.pallas import tpu as pltpu

F32 = jnp.float32
BF16 = jnp.bfloat16
I32 = jnp.int32

EPS = 1e-6
N_META = 16
HEAD_DIM = 64
N_Q_HEADS = 16
N_KV_HEADS = 2
GROUP = N_Q_HEADS // N_KV_HEADS
Q_DIM = N_Q_HEADS * HEAD_DIM
KV_DIM = N_KV_HEADS * HEAD_DIM
ATT_BLOCK = 128
LRU_WIDTH = 1024
LRU_BLOCKS = 16
LRU_BW = LRU_WIDTH // LRU_BLOCKS
LRU_C = 8.0
GATE_GROUP = 256
PEER_HEADS = 8
N_KEYS = 128
PEER_TOPK = 16
PEER_ROWS = PEER_HEADS * PEER_TOPK
NEG_BIG = -1e30

VMEM_LIMIT = 56 * 1024 * 1024


def _cparams(sem, vmem=VMEM_LIMIT):
    return pltpu.CompilerParams(dimension_semantics=sem, vmem_limit_bytes=vmem)


def _const_spec(shape):
    nd = len(shape)
    return pl.BlockSpec(shape, lambda *_: (0,) * nd, pipeline_mode=pl.Buffered(1))


def _rms(x, g):
    return x * lax.rsqrt(jnp.mean(x * x, axis=-1, keepdims=True) + EPS) * g


def _gelu(x):
    return jax.nn.gelu(x, approximate=True)


def _proj_kernel(x_ref, g_ref, w_ref, b_ref, qkv_ref, xb_ref, gate_ref):
    hn = _rms(x_ref[...], g_ref[...]).astype(BF16)
    n_qkv = Q_DIM + 2 * KV_DIM

    def proj(lo, hi):
        return jnp.dot(hn, w_ref[:, lo:hi], preferred_element_type=F32) + b_ref[:, lo:hi]

    step = 256
    for lo in range(0, n_qkv, step):
        qkv_ref[:, lo:lo + step] = proj(lo, lo + step).astype(BF16)
    for lo in range(0, LRU_WIDTH, step):
        xb_ref[:, lo:lo + step] = proj(n_qkv + lo, n_qkv + lo + step)
        gate_ref[:, lo:lo + step] = proj(n_qkv + LRU_WIDTH + lo, n_qkv + LRU_WIDTH + lo + step)


def _input_proj(x2, g, w_bf, b, tm):
    n, d = x2.shape
    n_qkv = Q_DIM + 2 * KV_DIM
    in_dim = w_bf.shape[1]
    return pl.pallas_call(
        _proj_kernel,
        grid=(n // tm,),
        in_specs=[
            pl.BlockSpec((tm, d), lambda i: (i, 0)),
            _const_spec((1, d)),
            _const_spec((d, in_dim)),
            _const_spec((1, in_dim)),
        ],
        out_specs=[
            pl.BlockSpec((tm, n_qkv), lambda i: (i, 0)),
            pl.BlockSpec((tm, LRU_WIDTH), lambda i: (i, 0)),
            pl.BlockSpec((tm, LRU_WIDTH), lambda i: (i, 0)),
        ],
        out_shape=[
            jax.ShapeDtypeStruct((n, n_qkv), BF16),
            jax.ShapeDtypeStruct((n, LRU_WIDTH), F32),
            jax.ShapeDtypeStruct((n, LRU_WIDTH), F32),
        ],
        compiler_params=_cparams(("parallel",)),
        name="input_proj",
    )(x2, g, w_bf, b)


def _attn_kernel(nb, q_ref, kc_ref, kp_ref, vc_ref, vp_ref, km_ref, vm_ref, sink_ref, g_ref,
                 o_ref, acc_ref):
    blk = pl.program_id(0) % nb
    row = lax.broadcasted_iota(I32, (ATT_BLOCK, 2 * ATT_BLOCK), 0)
    col = lax.broadcasted_iota(I32, (ATT_BLOCK, 2 * ATT_BLOCK), 1)
    no_prev = jnp.where(blk > 0, 0, 2 * ATT_BLOCK)
    in_prev = jnp.logical_and(col < ATT_BLOCK, col > row + no_prev)
    in_cur = jnp.logical_and(col >= ATT_BLOCK, (col - ATT_BLOCK) <= row)
    band = jnp.logical_or(in_prev, in_cur)
    scale = HEAD_DIM ** -0.5
    nt = (((1,), (1,)), ((), ()))
    for g in range(N_KV_HEADS):
        ks = slice(g * HEAD_DIM, (g + 1) * HEAD_DIM)
        kb = jnp.concatenate([kp_ref[:, ks], kc_ref[:, ks]], axis=0)
        vb = jnp.concatenate([vp_ref[:, ks], vc_ref[:, ks]], axis=0)
        kmg = km_ref[:, ks]
        vmg = vm_ref[:, ks]
        for hh in range(GROUP):
            h = g * GROUP + hh
            qh = q_ref[:, h * HEAD_DIM:(h + 1) * HEAD_DIM]
            sb = lax.dot_general(qh, kb, nt, preferred_element_type=F32) * scale
            sm = lax.dot_general(qh, kmg, nt, preferred_element_type=F32) * scale
            sb = jnp.where(band, sb, NEG_BIG)
            sink = sink_ref[h]
            m = jnp.maximum(jnp.max(sb, axis=-1, keepdims=True), jnp.max(sm, axis=-1, keepdims=True))
            m = jnp.maximum(m, sink)
            pb = jnp.exp(sb - m)
            pm = jnp.exp(sm - m)
            den = (jnp.sum(pb, axis=-1, keepdims=True) + jnp.sum(pm, axis=-1, keepdims=True)
                   + jnp.exp(sink - m))
            o = (jnp.dot(pb.astype(BF16), vb, preferred_element_type=F32)
                 + jnp.dot(pm.astype(BF16), vmg, preferred_element_type=F32))
            acc_ref[:, h * HEAD_DIM:(h + 1) * HEAD_DIM] = o / den
    o_ref[...] = _rms(acc_ref[...], g_ref[...]).astype(BF16)


def _attention(qkv, km, vm, sinks, g, n_batch):
    n = qkv.shape[0]
    nblk = n // ATT_BLOCK
    nb = nblk // n_batch
    kcol = Q_DIM // KV_DIM
    return pl.pallas_call(
        functools.partial(_attn_kernel, nb),
        grid=(nblk,),
        in_specs=[
            pl.BlockSpec((ATT_BLOCK, Q_DIM), lambda i: (i, 0)),
            pl.BlockSpec((ATT_BLOCK, KV_DIM), lambda i: (i, kcol)),
            pl.BlockSpec((ATT_BLOCK, KV_DIM), lambda i: (jnp.maximum(i - 1, 0), kcol)),
            pl.BlockSpec((ATT_BLOCK, KV_DIM), lambda i: (i, kcol + 1)),
            pl.BlockSpec((ATT_BLOCK, KV_DIM), lambda i: (jnp.maximum(i - 1, 0), kcol + 1)),
            _const_spec((N_META, KV_DIM)),
            _const_spec((N_META, KV_DIM)),
            pl.BlockSpec(memory_space=pltpu.SMEM),
            _const_spec((1, Q_DIM)),
        ],
        out_specs=pl.BlockSpec((ATT_BLOCK, Q_DIM), lambda i: (i, 0)),
        out_shape=jax.ShapeDtypeStruct((n, Q_DIM), BF16),
        scratch_shapes=[pltpu.VMEM((ATT_BLOCK, Q_DIM), F32)],
        compiler_params=_cparams(("parallel",)),
        name="swa_attention",
    )(qkv, qkv, qkv, qkv, qkv, km, vm, sinks, g)


def _log_sigmoid(x):
    return -(jnp.maximum(-x, 0.0) + jnp.log1p(jnp.exp(-jnp.abs(x))))


def _lru_kernel(tt, x_ref, gate_ref, cw_ref, cb_ref, wr_ref, br_ref, wi_ref, bi_ref, lam_ref, g_ref,
                h0_ref, tail0_ref, o_ref, hlast_ref, xfull_ref, h_ref):
    t = pl.program_id(1)

    @pl.when(t == 0)
    def _():
        h_ref[...] = h0_ref[...]
        xfull_ref[0:8, :] = tail0_ref[...]

    x = x_ref[...]
    xfull_ref[8:8 + tt, :] = x
    xc = (cb_ref[...]
          + cw_ref[0:1, :] * xfull_ref[5:5 + tt, :]
          + cw_ref[1:2, :] * xfull_ref[6:6 + tt, :]
          + cw_ref[2:3, :] * xfull_ref[7:7 + tt, :]
          + cw_ref[3:4, :] * x)
    xfull_ref[0:8, :] = x[tt - 8:tt, :]

    xcb = xc.astype(BF16)
    r_parts, i_parts = [], []
    for gi in range(LRU_WIDTH // GATE_GROUP):
        sl = slice(gi * GATE_GROUP, (gi + 1) * GATE_GROUP)
        r_parts.append(jnp.dot(xcb[:, sl], wr_ref[gi], preferred_element_type=F32))
        i_parts.append(jnp.dot(xcb[:, sl], wi_ref[gi], preferred_element_type=F32))
    r = jax.nn.sigmoid(jnp.concatenate(r_parts, axis=-1) + br_ref[...])
    ig = jax.nn.sigmoid(jnp.concatenate(i_parts, axis=-1) + bi_ref[...])
    log_a = LRU_C * r * _log_sigmoid(lam_ref[...])
    a = jnp.exp(log_a)
    th = jnp.tanh(log_a)
    b = jnp.sqrt(-2.0 * th / (1.0 - th)) * (ig * xc)

    rows = lax.broadcasted_iota(I32, a.shape, 0)
    s = 1
    while s < tt:
        live = rows >= s
        a_sh = pltpu.roll(a, s, 0)
        b_sh = pltpu.roll(b, s, 0)
        b = jnp.where(live, a * b_sh + b, b)
        a = jnp.where(live, a * a_sh, a)
        s *= 2
    h = b + a * h_ref[0:1, :]
    hl = h[tt - 1:tt, :]
    h_ref[...] = jnp.broadcast_to(hl, h_ref.shape)
    hlast_ref[...] = jnp.broadcast_to(hl, hlast_ref.shape)

    o_ref[...] = _rms(_gelu(gate_ref[...]) * h, g_ref[...]).astype(BF16)


def _lru(xb, gate, cw, cb, wr_bd, br, wi_bd, bi, lam, g, h0, tail0, n_batch, tt):
    n = xb.shape[0]
    nt = n // n_batch // tt
    w = LRU_WIDTH
    ng = w // GATE_GROUP
    row_spec = pl.BlockSpec((tt, w), lambda b, t: (b * nt + t, 0))
    return pl.pallas_call(
        functools.partial(_lru_kernel, tt),
        grid=(n_batch, nt),
        in_specs=[
            row_spec, row_spec,
            _const_spec((4, w)), _const_spec((1, w)),
            _const_spec((ng, GATE_GROUP, GATE_GROUP)), _const_spec((1, w)),
            _const_spec((ng, GATE_GROUP, GATE_GROUP)), _const_spec((1, w)),
            _const_spec((1, w)), _const_spec((1, w)),
            _const_spec((8, w)), _const_spec((8, w)),
        ],
        out_specs=[row_spec, pl.BlockSpec((8, w), lambda b, t: (b * nt + t, 0))],
        out_shape=[jax.ShapeDtypeStruct((n, w), BF16),
                   jax.ShapeDtypeStruct((n_batch * nt * 8, w), F32)],
        scratch_shapes=[pltpu.VMEM((tt + 8, w), F32), pltpu.VMEM((8, w), F32)],
        compiler_params=_cparams(("arbitrary", "arbitrary")),
        name="rg_lru",
    )(xb, gate, cw, cb, wr_bd, br, wi_bd, bi, lam, g, h0, tail0)


def _block_diag_groups(w):
    per = GATE_GROUP // LRU_BW
    w4 = w.reshape(LRU_BLOCKS // per, per, LRU_BW, LRU_BW)
    eye = jnp.eye(per, dtype=w.dtype)
    return jnp.einsum('gacd,ab->gacbd', w4, eye).reshape(LRU_BLOCKS // per, GATE_GROUP, GATE_GROUP)


def _mix_kernel(a_ref, l_ref, x_ref, wa_ref, wl_ref, g_ref, wq_ref, h_ref, xw_ref, q_ref):
    h = (x_ref[...]
         + jnp.dot(a_ref[...], wa_ref[...], preferred_element_type=F32)
         + jnp.dot(l_ref[...], wl_ref[...], preferred_element_type=F32))
    h_ref[...] = h
    hn = _rms(h, g_ref[...]).astype(BF16)
    xw_ref[...] = hn
    q_ref[...] = jnp.dot(hn, wq_ref[...], preferred_element_type=F32).astype(BF16)


def _mix(attn_n, lru_n, x2, wa, wl, g, wq, tm):
    n, d = x2.shape
    dq = wq.shape[1]
    return pl.pallas_call(
        _mix_kernel,
        grid=(n // tm,),
        in_specs=[
            pl.BlockSpec((tm, Q_DIM), lambda i: (i, 0)),
            pl.BlockSpec((tm, LRU_WIDTH), lambda i: (i, 0)),
            pl.BlockSpec((tm, d), lambda i: (i, 0)),
            _const_spec((Q_DIM, d)), _const_spec((LRU_WIDTH, d)), _const_spec((1, d)),
            _const_spec((d, dq)),
        ],
        out_specs=[
            pl.BlockSpec((tm, d), lambda i: (i, 0)),
            pl.BlockSpec((tm, d), lambda i: (i, 0)),
            pl.BlockSpec((tm, dq), lambda i: (i, 0)),
        ],
        out_shape=[
            jax.ShapeDtypeStruct((n, d), F32),
            jax.ShapeDtypeStruct((n, d), BF16),
            jax.ShapeDtypeStruct((n, dq), BF16),
        ],
        compiler_params=_cparams(("parallel",)),
        name="out_proj_peer_query",
    )(attn_n, lru_n, x2, wa, wl, g, wq)


def _top16_rows(s, ids):
    big = jnp.int32(2 ** 30)
    vals, picks = [], []
    for _ in range(PEER_TOPK):
        m = jnp.max(s, axis=0, keepdims=True)
        pick = jnp.min(jnp.where(s == m, ids, big), axis=0, keepdims=True)
        s = jnp.where(ids == pick, -jnp.inf, s)
        vals.append(m)
        picks.append(pick)
    return jnp.concatenate(vals, axis=0), jnp.concatenate(picks, axis=0)


def _retrieve_kernel(q_ref, keys_ref, e_ref, g_ref):
    tm = q_ref.shape[0]
    nt = (((1,), (1,)), ((), ()))
    key_ids = lax.broadcasted_iota(I32, (N_KEYS, tm), 0)
    tops = []
    for p in range(2):
        qp = q_ref[:, p * N_KEYS:(p + 1) * N_KEYS]
        s = lax.dot_general(keys_ref[0, p], qp, nt, preferred_element_type=F32)
        tops.append(_top16_rows(s, key_ids))
    (v0, i0), (v1, i1) = tops
    cand = [v0[0:1] + v1]
    cid = [lax.broadcasted_iota(I32, (PEER_TOPK, tm), 0)]
    eid = [i0[0:1] * N_KEYS + i1]
    for k1 in range(1, PEER_TOPK):
        cand.append(v0[k1:k1 + 1] + v1[0:8])
        cid.append(lax.broadcasted_iota(I32, (8, tm), 0) + k1 * PEER_TOPK)
        eid.append(i0[k1:k1 + 1] * N_KEYS + i1[0:8])
    cand = jnp.concatenate(cand, axis=0)
    cid = jnp.concatenate(cid, axis=0)
    eid = jnp.concatenate(eid, axis=0)
    best, experts = [], []
    big = jnp.int32(2 ** 30)
    for _ in range(PEER_TOPK):
        m = jnp.max(cand, axis=0, keepdims=True)
        pick = jnp.min(jnp.where(cand == m, cid, big), axis=0, keepdims=True)
        sel = cid == pick
        experts.append(jnp.max(jnp.where(sel, eid, -1), axis=0, keepdims=True))
        cand = jnp.where(sel, -jnp.inf, cand)
        best.append(m)
    best = jnp.concatenate(best, axis=0)
    ex = jnp.exp(best - best[0:1])
    g_ref[...] = ex / jnp.sum(ex, axis=0, keepdims=True)
    e_ref[...] = jnp.concatenate(experts, axis=0)


def _retrieve(q, keys_bf, tm):
    n = q.shape[0]
    dk = 2 * N_KEYS
    return pl.pallas_call(
        _retrieve_kernel,
        grid=(n // tm, PEER_HEADS),
        in_specs=[
            pl.BlockSpec((tm, dk), lambda i, h: (i, h)),
            pl.BlockSpec((1, 2, N_KEYS, N_KEYS), lambda i, h: (h, 0, 0, 0)),
        ],
        out_specs=[
            pl.BlockSpec((PEER_TOPK, tm), lambda i, h: (h, i)),
            pl.BlockSpec((PEER_TOPK, tm), lambda i, h: (h, i)),
        ],
        out_shape=[
            jax.ShapeDtypeStruct((PEER_ROWS, n), I32),
            jax.ShapeDtypeStruct((PEER_ROWS, n), F32),
        ],
        compiler_params=_cparams(("parallel", "arbitrary")),
        name="peer_retrieve",
    )(q, keys_bf)


def _cast_tables_kernel(u_ref, v_ref, o_ref):
    d = u_ref.shape[1]
    o_ref[:, :d] = u_ref[...].astype(BF16)
    o_ref[:, d:] = v_ref[...].astype(BF16)


def _cast_tables(u, v, tr):
    ne, d = u.shape
    return pl.pallas_call(
        _cast_tables_kernel,
        grid=(ne // tr,),
        in_specs=[pl.BlockSpec((tr, d), lambda i: (i, 0)), pl.BlockSpec((tr, d), lambda i: (i, 0))],
        out_specs=pl.BlockSpec((tr, 2 * d), lambda i: (i, 0)),
        out_shape=jax.ShapeDtypeStruct((ne, 2 * d), BF16),
        compiler_params=_cparams(("parallel",)),
        name="peer_cast_tables",
    )(u, v)


SUBLANES = 8
LANES = 128


def _fold_pair(x, y, sh, rows):
    x2 = x + pltpu.roll(x, sh, 0)
    y2 = y + pltpu.roll(y, SUBLANES - sh, 0)
    return jnp.where((rows // sh) % 2 == 1, x2, y2)


def _fold_order():
    rows = np.arange(SUBLANES)[:, None]

    def fold(x, y, sh):
        x2 = x + np.roll(x, sh, 0)
        y2 = y + np.roll(y, SUBLANES - sh, 0)
        return np.where((rows // sh) % 2 == 1, x2, y2)

    vals = []
    for i in range(SUBLANES):
        v = np.zeros((SUBLANES, SUBLANES * SUBLANES), np.int64)
        v[np.arange(SUBLANES), i * SUBLANES + np.arange(SUBLANES)] = 1
        vals.append(v)
    sh = SUBLANES // 2
    while len(vals) > 1:
        vals = [fold(vals[2 * k], vals[2 * k + 1], sh) for k in range(len(vals) // 2)]
        sh //= 2
    out = vals[0].reshape(SUBLANES, SUBLANES, SUBLANES)
    src_of_row = [int(np.argmax(out[r].sum(-1))) for r in range(SUBLANES)]
    for r in range(SUBLANES):
        assert (out[r, src_of_row[r]] == 1).all() and out[r].sum() == SUBLANES
    order = [0] * SUBLANES
    for r, p in enumerate(src_of_row):
        order[p] = r
    return tuple(order)


_FOLD_ORDER = _fold_order()


def _sublane_sums(parts, rows):
    vals = [parts[r] for r in _FOLD_ORDER]
    sh = SUBLANES // 2
    while len(vals) > 1:
        vals = [_fold_pair(vals[2 * k], vals[2 * k + 1], sh, rows) for k in range(len(vals) // 2)]
        sh //= 2
    return vals[0]


def _expert_kernel(tb, lane_blk, idx_ref, idxn_ref, tab_ref, xw_ref, gw_ref, h_ref, gf_ref,
                   y_ref, buf_a_ref, buf_b_ref, sem_ref, wrep_ref):
    i = pl.program_id(0)
    nsteps = pl.num_programs(0)
    rows_per_half = tb * PEER_ROWS
    ds = xw_ref.shape[1]
    bufs = (buf_a_ref, buf_b_ref)

    def row_copy(e, half, r):
        return pltpu.make_async_copy(tab_ref.at[e], bufs[half].at[r], sem_ref.at[half])

    def half_wait(half):
        pltpu.make_async_copy(tab_ref.at[pl.ds(0, rows_per_half)], bufs[half], sem_ref.at[half]).wait()

    @pl.when(i == 0)
    def _():
        def body(t, carry):
            for j in range(PEER_ROWS):
                row_copy(idx_ref[t, j], 0, t * PEER_ROWS + j).start()
            return carry
        lax.fori_loop(0, tb, body, 0)

    rows8 = lax.broadcasted_iota(I32, (SUBLANES, LANES), 0)
    lane = lax.broadcasted_iota(I32, (PEER_ROWS, lane_blk), 1)
    half_rows = PEER_ROWS // 2

    def process(half, next_id):
        other = 1 - half
        t0 = half * tb
        half_wait(half)
        buf = bufs[half]
        base = (i * 2 * tb + t0) % lane_blk

        def prefetch(t, j):
            row_copy(next_id(t, j), other, t * PEER_ROWS + j).start(priority=j % 2)

        pre = jnp.zeros((PEER_ROWS, lane_blk), F32)
        for t in range(tb):
            x_t = xw_ref[t0 + t]
            sums = []
            for g0 in range(0, PEER_ROWS, SUBLANES):
                for j in range(g0 // 2, (g0 + SUBLANES) // 2):
                    prefetch(t, j)
                parts = []
                for j in range(g0, g0 + SUBLANES):
                    p = (buf[t * PEER_ROWS + j, 0:ds, :] * x_t).astype(F32)
                    parts.append(p[0:SUBLANES] + p[SUBLANES:2 * SUBLANES])
                sums.append(_sublane_sums(parts, rows8))
            rowsum = jnp.sum(jnp.concatenate(sums, axis=0), axis=-1, keepdims=True)
            pre = jnp.where(lane == base + t, rowsum, pre)
        w = gw_ref[...] * _gelu(pre)
        for t in range(tb):
            wcol = jnp.sum(jnp.where(lane == base + t, w, 0.0), axis=-1, keepdims=True)
            wrep_ref[t] = jnp.broadcast_to(wcol, (PEER_ROWS, LANES))
        n_acc = 4
        for t in range(tb):
            accs = [jnp.zeros((ds, LANES), F32) for _ in range(n_acc)]
            for j in range(PEER_ROWS):
                if j % 2 == 0:
                    prefetch(t, half_rows + j // 2)
                v = buf[t * PEER_ROWS + j, ds:2 * ds, :].astype(F32)
                accs[j % n_acc] = accs[j % n_acc] + v * wrep_ref[t, j:j + 1, :]
            hp = h_ref[t0 + t] + ((accs[0] + accs[1]) + (accs[2] + accs[3]))
            ms = jnp.mean(jnp.mean(hp * hp, axis=-1, keepdims=True), axis=0, keepdims=True)
            y_ref[t0 + t] = hp * lax.rsqrt(ms + EPS) * gf_ref[...]

    process(0, lambda t, j: idx_ref[tb + t, j])
    process(1, lambda t, j: idxn_ref[t, j])

    @pl.when(i == nsteps - 1)
    def _():
        half_wait(0)


def _experts(idx, tab3, xw3, gw, h3, gf3, tb):
    n, ds, lanes = h3.shape
    lane_blk = min(n, LANES)
    step_tokens = 2 * tb
    nsteps = n // step_tokens
    per_blk = lane_blk // step_tokens
    buf = pltpu.VMEM((tb * PEER_ROWS, 2 * ds, lanes), BF16)
    return pl.pallas_call(
        functools.partial(_expert_kernel, tb, lane_blk),
        grid=(nsteps,),
        in_specs=[
            pl.BlockSpec((step_tokens, PEER_ROWS), lambda i: (i, 0), memory_space=pltpu.SMEM),
            pl.BlockSpec((tb, PEER_ROWS), lambda i: (2 * jnp.minimum(i + 1, nsteps - 1), 0),
                         memory_space=pltpu.SMEM),
            pl.BlockSpec(memory_space=pl.ANY),
            pl.BlockSpec((step_tokens, ds, lanes), lambda i: (i, 0, 0)),
            pl.BlockSpec((PEER_ROWS, lane_blk), lambda i: (0, i // per_blk)),
            pl.BlockSpec((step_tokens, ds, lanes), lambda i: (i, 0, 0)),
            _const_spec((ds, lanes)),
        ],
        out_specs=pl.BlockSpec((step_tokens, ds, lanes), lambda i: (i, 0, 0)),
        out_shape=jax.ShapeDtypeStruct((n, ds, lanes), F32),
        scratch_shapes=[
            buf, buf,
            pltpu.SemaphoreType.DMA((2,)),
            pltpu.VMEM((tb, PEER_ROWS, LANES), F32),
        ],
        compiler_params=_cparams(("arbitrary",)),
        name="peer_experts",
    )(idx, idx, tab3, xw3, gw, h3, gf3)


def _row_tile(n, pref):
    t = pref
    while n % t:
        t //= 2
    return t


def kernel(x, meta_tokens, norm_mix_g, w_in, b_in, sinks, conv_w, conv_b, w_r, b_r, w_i, b_i, lru_lambda, gn_attn_g, gn_lru_g, w_out, norm_ffn_g, peer_wq, peer_sub_keys, peer_u, peer_v, final_norm_g):
    n_batch, seq, d = x.shape
    n = n_batch * seq
    assert w_in.shape[0] == 1 and seq % ATT_BLOCK == 0
    x2 = x.reshape(n, d)
    row = lambda a: a.reshape(1, -1)

    w_in_bf = w_in[0].astype(BF16)
    g_mix = row(norm_mix_g[0])
    b_in_r = row(b_in[0])
    qkv, xb, gate = _input_proj(x2, g_mix, w_in_bf, b_in_r, _row_tile(n, 256))
    qkv_m, xb_m, gate_m = _input_proj(meta_tokens.astype(F32), g_mix, w_in_bf, b_in_r, N_META)

    km = qkv_m[:, Q_DIM:Q_DIM + KV_DIM]
    vm = qkv_m[:, Q_DIM + KV_DIM:]
    attn_n = _attention(qkv, km, vm, sinks[0].astype(F32), row(gn_attn_g[0]), n_batch)

    lru_args = (conv_w[0], row(conv_b[0]), _block_diag_groups(w_r[0]).astype(BF16), row(b_r[0]),
                _block_diag_groups(w_i[0]).astype(BF16), row(b_i[0]), row(lru_lambda[0]),
                row(gn_lru_g[0]))
    zeros8 = jnp.zeros((8, LRU_WIDTH), F32)
    _, h_meta = _lru(xb_m, gate_m, *lru_args, zeros8, zeros8, 1, N_META)
    lru_n, _ = _lru(xb, gate, *lru_args, h_meta, xb_m[N_META - 8:], n_batch, _row_tile(seq, 256))

    w_out_bf = w_out[0].astype(BF16)
    h1, xw, q = _mix(attn_n, lru_n, x2, w_out_bf[:Q_DIM], w_out_bf[Q_DIM:], row(norm_ffn_g[0]),
                     peer_wq[0].astype(BF16), _row_tile(n, 256))

    experts, gw = _retrieve(q, peer_sub_keys[0].astype(BF16), _row_tile(n, 256))
    ne = peer_u.shape[1]
    ds = d // LANES
    tab3 = _cast_tables(peer_u[0], peer_v[0], 256).reshape(ne, 2 * ds, LANES)
    y3 = _experts(experts.T, tab3, xw.reshape(n, ds, LANES), gw, h1.reshape(n, ds, LANES),
                  final_norm_g.reshape(ds, LANES), 8)
    return y3.reshape(n_batch, seq, d)
```

```python
import functools

import numpy as np
import jax
import jax.numpy as jnp
from jax import lax
from jax.experimental import pallas as pl
from jax.experimental.pallas import tpu as pltpu

F32 = jnp.float32
BF16 = jnp.bfloat16
I32 = jnp.int32

EPS = 1e-6
N_META = 16
HEAD_DIM = 64
N_Q_HEADS = 16
N_KV_HEADS = 2
GROUP = N_Q_HEADS // N_KV_HEADS
Q_DIM = N_Q_HEADS * HEAD_DIM
KV_DIM = N_KV_HEADS * HEAD_DIM
ATT_BLOCK = 128
LRU_WIDTH = 1024
LRU_BLOCKS = 16
LRU_BW = LRU_WIDTH // LRU_BLOCKS
LRU_C = 8.0
GATE_GROUP = 256
PEER_HEADS = 8
N_KEYS = 128
PEER_TOPK = 16
PEER_ROWS = PEER_HEADS * PEER_TOPK
NEG_BIG = -1e30

VMEM_LIMIT = 56 * 1024 * 1024


def _cparams(sem, vmem=VMEM_LIMIT):
    return pltpu.CompilerParams(dimension_semantics=sem, vmem_limit_bytes=vmem)


def _const_spec(shape):
    nd = len(shape)
    return pl.BlockSpec(shape, lambda *_: (0,) * nd, pipeline_mode=pl.Buffered(1))


def _rms(x, g):
    return x * lax.rsqrt(jnp.mean(x * x, axis=-1, keepdims=True) + EPS) * g


def _gelu(x):
    return jax.nn.gelu(x, approximate=True)


def _proj_kernel(x_ref, g_ref, w_ref, b_ref, qkv_ref, xb_ref, gate_ref):
    hn = _rms(x_ref[...], g_ref[...]).astype(BF16)
    n_qkv = Q_DIM + 2 * KV_DIM

    def proj(lo, hi):
        return jnp.dot(hn, w_ref[:, lo:hi], preferred_element_type=F32) + b_ref[:, lo:hi]

    step = 256
    for lo in range(0, n_qkv, step):
        qkv_ref[:, lo:lo + step] = proj(lo, lo + step).astype(BF16)
    for lo in range(0, LRU_WIDTH, step):
        xb_ref[:, lo:lo + step] = proj(n_qkv + lo, n_qkv + lo + step)
        gate_ref[:, lo:lo + step] = proj(n_qkv + LRU_WIDTH + lo, n_qkv + LRU_WIDTH + lo + step)


def _input_proj(x2, g, w_bf, b, tm):
    n, d = x2.shape
    n_qkv = Q_DIM + 2 * KV_DIM
    in_dim = w_bf.shape[1]
    return pl.pallas_call(
        _proj_kernel,
        grid=(n // tm,),
        in_specs=[
            pl.BlockSpec((tm, d), lambda i: (i, 0)),
            _const_spec((1, d)),
            _const_spec((d, in_dim)),
            _const_spec((1, in_dim)),
        ],
        out_specs=[
            pl.BlockSpec((tm, n_qkv), lambda i: (i, 0)),
            pl.BlockSpec((tm, LRU_WIDTH), lambda i: (i, 0)),
            pl.BlockSpec((tm, LRU_WIDTH), lambda i: (i, 0)),
        ],
        out_shape=[
            jax.ShapeDtypeStruct((n, n_qkv), BF16),
            jax.ShapeDtypeStruct((n, LRU_WIDTH), F32),
            jax.ShapeDtypeStruct((n, LRU_WIDTH), F32),
        ],
        compiler_params=_cparams(("parallel",)),
        name="input_proj",
    )(x2, g, w_bf, b)


def _attn_kernel(nb, q_ref, kc_ref, kp_ref, vc_ref, vp_ref, km_ref, vm_ref, sink_ref, g_ref,
                 o_ref, acc_ref):
    blk = pl.program_id(0) % nb
    row = lax.broadcasted_iota(I32, (ATT_BLOCK, 2 * ATT_BLOCK), 0)
    col = lax.broadcasted_iota(I32, (ATT_BLOCK, 2 * ATT_BLOCK), 1)
    no_prev = jnp.where(blk > 0, 0, 2 * ATT_BLOCK)
    in_prev = jnp.logical_and(col < ATT_BLOCK, col > row + no_prev)
    in_cur = jnp.logical_and(col >= ATT_BLOCK, (col - ATT_BLOCK) <= row)
    band = jnp.logical_or(in_prev, in_cur)
    scale = HEAD_DIM ** -0.5
    nt = (((1,), (1,)), ((), ()))
    for g in range(N_KV_HEADS):
        ks = slice(g * HEAD_DIM, (g + 1) * HEAD_DIM)
        kb = jnp.concatenate([kp_ref[:, ks], kc_ref[:, ks]], axis=0)
        vb = jnp.concatenate([vp_ref[:, ks], vc_ref[:, ks]], axis=0)
        kmg = km_ref[:, ks]
        vmg = vm_ref[:, ks]
        for hh in range(GROUP):
            h = g * GROUP + hh
            qh = q_ref[:, h * HEAD_DIM:(h + 1) * HEAD_DIM]
            sb = lax.dot_general(qh, kb, nt, preferred_element_type=F32) * scale
            sm = lax.dot_general(qh, kmg, nt, preferred_element_type=F32) * scale
            sb = jnp.where(band, sb, NEG_BIG)
            sink = sink_ref[h]
            m = jnp.maximum(jnp.max(sb, axis=-1, keepdims=True), jnp.max(sm, axis=-1, keepdims=True))
            m = jnp.maximum(m, sink)
            pb = jnp.exp(sb - m)
            pm = jnp.exp(sm - m)
            den = (jnp.sum(pb, axis=-1, keepdims=True) + jnp.sum(pm, axis=-1, keepdims=True)
                   + jnp.exp(sink - m))
            o = (jnp.dot(pb.astype(BF16), vb, preferred_element_type=F32)
                 + jnp.dot(pm.astype(BF16), vmg, preferred_element_type=F32))
            acc_ref[:, h * HEAD_DIM:(h + 1) * HEAD_DIM] = o / den
    o_ref[...] = _rms(acc_ref[...], g_ref[...]).astype(BF16)


def _attention(qkv, km, vm, sinks, g, n_batch):
    n = qkv.shape[0]
    nblk = n // ATT_BLOCK
    nb = nblk // n_batch
    kcol = Q_DIM // KV_DIM
    return pl.pallas_call(
        functools.partial(_attn_kernel, nb),
        grid=(nblk,),
        in_specs=[
            pl.BlockSpec((ATT_BLOCK, Q_DIM), lambda i: (i, 0)),
            pl.BlockSpec((ATT_BLOCK, KV_DIM), lambda i: (i, kcol)),
            pl.BlockSpec((ATT_BLOCK, KV_DIM), lambda i: (jnp.maximum(i - 1, 0), kcol)),
            pl.BlockSpec((ATT_BLOCK, KV_DIM), lambda i: (i, kcol + 1)),
            pl.BlockSpec((ATT_BLOCK, KV_DIM), lambda i: (jnp.maximum(i - 1, 0), kcol + 1)),
            _const_spec((N_META, KV_DIM)),
            _const_spec((N_META, KV_DIM)),
            pl.BlockSpec(memory_space=pltpu.SMEM),
            _const_spec((1, Q_DIM)),
        ],
        out_specs=pl.BlockSpec((ATT_BLOCK, Q_DIM), lambda i: (i, 0)),
        out_shape=jax.ShapeDtypeStruct((n, Q_DIM), BF16),
        scratch_shapes=[pltpu.VMEM((ATT_BLOCK, Q_DIM), F32)],
        compiler_params=_cparams(("parallel",)),
        name="swa_attention",
    )(qkv, qkv, qkv, qkv, qkv, km, vm, sinks, g)


def _log_sigmoid(x):
    return -(jnp.maximum(-x, 0.0) + jnp.log1p(jnp.exp(-jnp.abs(x))))


def _lru_kernel(tt, x_ref, gate_ref, cw_ref, cb_ref, wr_ref, br_ref, wi_ref, bi_ref, lam_ref, g_ref,
                h0_ref, tail0_ref, o_ref, hlast_ref, xfull_ref, h_ref):
    t = pl.program_id(1)

    @pl.when(t == 0)
    def _():
        h_ref[...] = h0_ref[...]
        xfull_ref[0:8, :] = tail0_ref[...]

    x = x_ref[...]
    xfull_ref[8:8 + tt, :] = x
    xc = (cb_ref[...]
          + cw_ref[0:1, :] * xfull_ref[5:5 + tt, :]
          + cw_ref[1:2, :] * xfull_ref[6:6 + tt, :]
          + cw_ref[2:3, :] * xfull_ref[7:7 + tt, :]
          + cw_ref[3:4, :] * x)
    xfull_ref[0:8, :] = x[tt - 8:tt, :]

    xcb = xc.astype(BF16)
    r_parts, i_parts = [], []
    for gi in range(LRU_WIDTH // GATE_GROUP):
        sl = slice(gi * GATE_GROUP, (gi + 1) * GATE_GROUP)
        r_parts.append(jnp.dot(xcb[:, sl], wr_ref[gi], preferred_element_type=F32))
        i_parts.append(jnp.dot(xcb[:, sl], wi_ref[gi], preferred_element_type=F32))
    r = jax.nn.sigmoid(jnp.concatenate(r_parts, axis=-1) + br_ref[...])
    ig = jax.nn.sigmoid(jnp.concatenate(i_parts, axis=-1) + bi_ref[...])
    log_a = LRU_C * r * _log_sigmoid(lam_ref[...])
    a = jnp.exp(log_a)
    th = jnp.tanh(log_a)
    b = jnp.sqrt(-2.0 * th / (1.0 - th)) * (ig * xc)

    rows = lax.broadcasted_iota(I32, a.shape, 0)
    s = 1
    while s < tt:
        live = rows >= s
        a_sh = pltpu.roll(a, s, 0)
        b_sh = pltpu.roll(b, s, 0)
        b = jnp.where(live, a * b_sh + b, b)
        a = jnp.where(live, a * a_sh, a)
        s *= 2
    h = b + a * h_ref[0:1, :]
    hl = h[tt - 1:tt, :]
    h_ref[...] = jnp.broadcast_to(hl, h_ref.shape)
    hlast_ref[...] = jnp.broadcast_to(hl, hlast_ref.shape)

    o_ref[...] = _rms(_gelu(gate_ref[...]) * h, g_ref[...]).astype(BF16)


def _lru(xb, gate, cw, cb, wr_bd, br, wi_bd, bi, lam, g, h0, tail0, n_batch, tt):
    n = xb.shape[0]
    nt = n // n_batch // tt
    w = LRU_WIDTH
    ng = w // GATE_GROUP
    row_spec = pl.BlockSpec((tt, w), lambda b, t: (b * nt + t, 0))
    return pl.pallas_call(
        functools.partial(_lru_kernel, tt),
        grid=(n_batch, nt),
        in_specs=[
            row_spec, row_spec,
            _const_spec((4, w)), _const_spec((1, w)),
            _const_spec((ng, GATE_GROUP, GATE_GROUP)), _const_spec((1, w)),
            _const_spec((ng, GATE_GROUP, GATE_GROUP)), _const_spec((1, w)),
            _const_spec((1, w)), _const_spec((1, w)),
            _const_spec((8, w)), _const_spec((8, w)),
        ],
        out_specs=[row_spec, pl.BlockSpec((8, w), lambda b, t: (b * nt + t, 0))],
        out_shape=[jax.ShapeDtypeStruct((n, w), BF16),
                   jax.ShapeDtypeStruct((n_batch * nt * 8, w), F32)],
        scratch_shapes=[pltpu.VMEM((tt + 8, w), F32), pltpu.VMEM((8, w), F32)],
        compiler_params=_cparams(("arbitrary", "arbitrary")),
        name="rg_lru",
    )(xb, gate, cw, cb, wr_bd, br, wi_bd, bi, lam, g, h0, tail0)


def _block_diag_groups(w):
    per = GATE_GROUP // LRU_BW
    w4 = w.reshape(LRU_BLOCKS // per, per, LRU_BW, LRU_BW)
    eye = jnp.eye(per, dtype=w.dtype)
    return jnp.einsum('gacd,ab->gacbd', w4, eye).reshape(LRU_BLOCKS // per, GATE_GROUP, GATE_GROUP)


def _mix_kernel(a_ref, l_ref, x_ref, wa_ref, wl_ref, g_ref, wq_ref, h_ref, xw_ref, q_ref):
    h = (x_ref[...]
         + jnp.dot(a_ref[...], wa_ref[...], preferred_element_type=F32)
         + jnp.dot(l_ref[...], wl_ref[...], preferred_element_type=F32))
    h_ref[...] = h
    hn = _rms(h, g_ref[...]).astype(BF16)
    xw_ref[...] = hn
    q = jnp.dot(hn, wq_ref[...], preferred_element_type=F32).astype(BF16)
    dk = q_ref.shape[2]
    for hd in range(q_ref.shape[0]):
        q_ref[hd] = q[:, hd * dk:(hd + 1) * dk]


def _mix(attn_n, lru_n, x2, wa, wl, g, wq, tm):
    n, d = x2.shape
    dq = wq.shape[1]
    return pl.pallas_call(
        _mix_kernel,
        grid=(n // tm,),
        in_specs=[
            pl.BlockSpec((tm, Q_DIM), lambda i: (i, 0)),
            pl.BlockSpec((tm, LRU_WIDTH), lambda i: (i, 0)),
            pl.BlockSpec((tm, d), lambda i: (i, 0)),
            _const_spec((Q_DIM, d)), _const_spec((LRU_WIDTH, d)), _const_spec((1, d)),
            _const_spec((d, dq)),
        ],
        out_specs=[
            pl.BlockSpec((tm, d), lambda i: (i, 0)),
            pl.BlockSpec((tm, d), lambda i: (i, 0)),
            pl.BlockSpec((PEER_HEADS, tm, dq // PEER_HEADS), lambda i: (0, i, 0)),
        ],
        out_shape=[
            jax.ShapeDtypeStruct((n, d), F32),
            jax.ShapeDtypeStruct((n, d), BF16),
            jax.ShapeDtypeStruct((PEER_HEADS, n, dq // PEER_HEADS), BF16),
        ],
        compiler_params=_cparams(("parallel",)),
        name="out_proj_peer_query",
    )(attn_n, lru_n, x2, wa, wl, g, wq)


def _top16_rows(s, ids):
    vals, picks = [], []
    for _ in range(PEER_TOPK):
        m = jnp.max(s, axis=0, keepdims=True)
        pick = jnp.min(jnp.where(s == m, ids, jnp.inf), axis=0, keepdims=True)
        s = jnp.where(ids == pick, -jnp.inf, s)
        vals.append(m)
        picks.append(pick)
    return jnp.concatenate(vals, axis=0), jnp.concatenate(picks, axis=0)


def _peer_select(qh, keys):
    tm = qh.shape[0]
    nt = (((1,), (1,)), ((), ()))
    key_ids = lax.broadcasted_iota(I32, (N_KEYS, tm), 0).astype(F32)
    tops = []
    for p in range(2):
        qp = qh[:, p * N_KEYS:(p + 1) * N_KEYS]
        s = lax.dot_general(keys[p], qp, nt, preferred_element_type=F32)
        tops.append(_top16_rows(s, key_ids))
    (v0, i0), (v1, i1) = tops
    sub8 = lax.broadcasted_iota(I32, (8, tm), 0).astype(F32)
    cand = [v0[0:1] + v1]
    cid = [lax.broadcasted_iota(I32, (PEER_TOPK, tm), 0).astype(F32)]
    eid = [i0[0:1] * N_KEYS + i1]
    for k1 in range(1, PEER_TOPK):
        cand.append(v0[k1:k1 + 1] + v1[0:8])
        cid.append(sub8 + float(k1 * PEER_TOPK))
        eid.append(i0[k1:k1 + 1] * N_KEYS + i1[0:8])
    cand = jnp.concatenate(cand, axis=0)
    cid = jnp.concatenate(cid, axis=0)
    eid = jnp.concatenate(eid, axis=0)
    best, experts = [], []
    for _ in range(PEER_TOPK):
        m = jnp.max(cand, axis=0, keepdims=True)
        pick = jnp.min(jnp.where(cand == m, cid, jnp.inf), axis=0, keepdims=True)
        sel = cid == pick
        experts.append(jnp.max(jnp.where(sel, eid, -1.0), axis=0, keepdims=True))
        cand = jnp.where(sel, -jnp.inf, cand)
        best.append(m)
    best = jnp.concatenate(best, axis=0)
    ex = jnp.exp(best - best[0:1])
    return jnp.concatenate(experts, axis=0), ex / jnp.sum(ex, axis=0, keepdims=True)


def _cast_tables_kernel(u_ref, v_ref, o_ref):
    d = u_ref.shape[1]
    o_ref[:, :d] = u_ref[...].astype(BF16)
    o_ref[:, d:] = v_ref[...].astype(BF16)


def _cast_tables(u, v, tr):
    ne, d = u.shape
    return pl.pallas_call(
        _cast_tables_kernel,
        grid=(ne // tr,),
        in_specs=[pl.BlockSpec((tr, d), lambda i: (i, 0)), pl.BlockSpec((tr, d), lambda i: (i, 0))],
        out_specs=pl.BlockSpec((tr, 2 * d), lambda i: (i, 0)),
        out_shape=jax.ShapeDtypeStruct((ne, 2 * d), BF16),
        compiler_params=_cparams(("parallel",)),
        name="peer_cast_tables",
    )(u, v)


SUBLANES = 8
LANES = 128


def _fold_pair(x, y, sh, rows):
    x2 = x + pltpu.roll(x, sh, 0)
    y2 = y + pltpu.roll(y, SUBLANES - sh, 0)
    return jnp.where((rows // sh) % 2 == 1, x2, y2)


def _fold_order():
    rows = np.arange(SUBLANES)[:, None]

    def fold(x, y, sh):
        x2 = x + np.roll(x, sh, 0)
        y2 = y + np.roll(y, SUBLANES - sh, 0)
        return np.where((rows // sh) % 2 == 1, x2, y2)

    vals = []
    for i in range(SUBLANES):
        v = np.zeros((SUBLANES, SUBLANES * SUBLANES), np.int64)
        v[np.arange(SUBLANES), i * SUBLANES + np.arange(SUBLANES)] = 1
        vals.append(v)
    sh = SUBLANES // 2
    while len(vals) > 1:
        vals = [fold(vals[2 * k], vals[2 * k + 1], sh) for k in range(len(vals) // 2)]
        sh //= 2
    out = vals[0].reshape(SUBLANES, SUBLANES, SUBLANES)
    src_of_row = [int(np.argmax(out[r].sum(-1))) for r in range(SUBLANES)]
    for r in range(SUBLANES):
        assert (out[r, src_of_row[r]] == 1).all() and out[r].sum() == SUBLANES
    order = [0] * SUBLANES
    for r, p in enumerate(src_of_row):
        order[p] = r
    return tuple(order)


_FOLD_ORDER = _fold_order()


def _sublane_sums(parts, rows):
    vals = [parts[r] for r in _FOLD_ORDER]
    sh = SUBLANES // 2
    while len(vals) > 1:
        vals = [_fold_pair(vals[2 * k], vals[2 * k + 1], sh, rows) for k in range(len(vals) // 2)]
        sh //= 2
    return vals[0]


TOKEN_BLOCK = 128


def _expert_kernel(tb, q0_ref, qn_ref, keys_ref, tab_ref, xw_ref, h_ref, gf_ref,
                   y_ref, buf_a_ref, buf_b_ref, sem_ref, wrep_ref, e_ref, g_ref, idxv_ref, idx_ref,
                   isem_ref):
    i = pl.program_id(0)
    nsteps = pl.num_programs(0)
    step_tokens = 2 * tb
    steps_per_blk = TOKEN_BLOCK // step_tokens
    assert steps_per_blk == PEER_HEADS
    blk = i // steps_per_blk
    sub = i % steps_per_blk
    nblk = nsteps // steps_per_blk
    rows_per_half = tb * PEER_ROWS
    ds = xw_ref.shape[1]
    bufs = (buf_a_ref, buf_b_ref)
    tail = TOKEN_BLOCK - step_tokens

    def row_copy(e, half, r):
        return pltpu.make_async_copy(tab_ref.at[e], bufs[half].at[r], sem_ref.at[half])

    def half_wait(half):
        pltpu.make_async_copy(tab_ref.at[pl.ds(0, rows_per_half)], bufs[half], sem_ref.at[half]).wait()

    def select_head(q_ref, hd, slot):
        experts, gates = _peer_select(q_ref[hd], keys_ref[hd])
        r0 = pl.multiple_of(hd * PEER_TOPK, PEER_TOPK)
        e_ref[pl.ds(r0, PEER_TOPK), :] = experts
        g_ref[slot, pl.ds(r0, PEER_TOPK), :] = gates

    def ids_to_smem(lo, hi):
        cp = pltpu.make_async_copy(idxv_ref.at[pl.ds(lo, hi - lo)], idx_ref.at[pl.ds(lo, hi - lo)], isem_ref)
        cp.start()
        cp.wait()

    @pl.when(i == 0)
    def _():
        def head_body(hd, carry):
            select_head(q0_ref, hd, 0)
            return carry
        lax.fori_loop(0, PEER_HEADS, head_body, 0)
        idxv_ref[...] = jnp.transpose(e_ref[...]).astype(I32)
        ids_to_smem(0, TOKEN_BLOCK)

        def body(t, carry):
            for j in range(PEER_ROWS):
                row_copy(idx_ref[t, j], 0, t * PEER_ROWS + j).start()
            return carry
        lax.fori_loop(0, tb, body, 0)

    @pl.when(jnp.logical_and(sub == 0, blk > 0))
    def _():
        ids_to_smem(tail, TOKEN_BLOCK)

    @pl.when(blk + 1 < nblk)
    def _():
        select_head(qn_ref, sub, (blk + 1) % 2)

    @pl.when(jnp.logical_and(sub == steps_per_blk - 1, blk + 1 < nblk))
    def _():
        idxv_ref[...] = jnp.transpose(e_ref[...]).astype(I32)
        ids_to_smem(0, tail)

    rows8 = lax.broadcasted_iota(I32, (SUBLANES, LANES), 0)
    lane = lax.broadcasted_iota(I32, (PEER_ROWS, TOKEN_BLOCK), 1)
    half_rows = PEER_ROWS // 2
    gslot = blk % 2

    def process(half, next_row0):
        other = 1 - half
        t0 = half * tb
        half_wait(half)
        buf = bufs[half]
        base = sub * step_tokens + t0

        def prefetch(t, j):
            row_copy(idx_ref[next_row0 + t, j], other, t * PEER_ROWS + j).start(priority=j % 2)

        pre = jnp.zeros((PEER_ROWS, TOKEN_BLOCK), F32)
        for t in range(tb):
            x_t = xw_ref[t0 + t]
            sums = []
            for g0 in range(0, PEER_ROWS, SUBLANES):
                for j in range(g0 // 2, (g0 + SUBLANES) // 2):
                    prefetch(t, j)
                parts = []
                for j in range(g0, g0 + SUBLANES):
                    p = (buf[t * PEER_ROWS + j, 0:ds, :] * x_t).astype(F32)
                    parts.append(p[0:SUBLANES] + p[SUBLANES:2 * SUBLANES])
                sums.append(_sublane_sums(parts, rows8))
            rowsum = jnp.sum(jnp.concatenate(sums, axis=0), axis=-1, keepdims=True)
            pre = jnp.where(lane == base + t, rowsum, pre)
        w = g_ref[gslot] * _gelu(pre)
        for t in range(tb):
            wcol = jnp.sum(jnp.where(lane == base + t, w, 0.0), axis=-1, keepdims=True)
            wrep_ref[t] = jnp.broadcast_to(wcol, (PEER_ROWS, LANES))
        n_acc = 4
        for t in range(tb):
            accs = [jnp.zeros((ds, LANES), F32) for _ in range(n_acc)]
            for j in range(PEER_ROWS):
                if j % 2 == 0:
                    prefetch(t, half_rows + j // 2)
                v = buf[t * PEER_ROWS + j, ds:2 * ds, :].astype(F32)
                accs[j % n_acc] = accs[j % n_acc] + v * wrep_ref[t, j:j + 1, :]
            hp = h_ref[t0 + t] + ((accs[0] + accs[1]) + (accs[2] + accs[3]))
            ms = jnp.mean(jnp.mean(hp * hp, axis=-1, keepdims=True), axis=0, keepdims=True)
            y_ref[t0 + t] = hp * lax.rsqrt(ms + EPS) * gf_ref[...]

    process(0, sub * step_tokens + tb)
    process(1, ((sub + 1) % steps_per_blk) * step_tokens)

    @pl.when(i == nsteps - 1)
    def _():
        half_wait(0)


def _experts(q3, keys_bf, tab3, xw3, h3, gf3, tb):
    n, ds, lanes = h3.shape
    assert n % TOKEN_BLOCK == 0
    step_tokens = 2 * tb
    nsteps = n // step_tokens
    steps_per_blk = TOKEN_BLOCK // step_tokens
    nblk = n // TOKEN_BLOCK
    dk = q3.shape[2]
    buf = pltpu.VMEM((tb * PEER_ROWS, 2 * ds, lanes), BF16)
    return pl.pallas_call(
        functools.partial(_expert_kernel, tb),
        grid=(nsteps,),
        in_specs=[
            pl.BlockSpec((PEER_HEADS, TOKEN_BLOCK, dk), lambda i: (0, 0, 0)),
            pl.BlockSpec((PEER_HEADS, TOKEN_BLOCK, dk),
                         lambda i: (0, jnp.minimum(i // steps_per_blk + 1, nblk - 1), 0)),
            _const_spec((PEER_HEADS, 2, N_KEYS, N_KEYS)),
            pl.BlockSpec(memory_space=pl.ANY),
            pl.BlockSpec((step_tokens, ds, lanes), lambda i: (i, 0, 0)),
            pl.BlockSpec((step_tokens, ds, lanes), lambda i: (i, 0, 0)),
            _const_spec((ds, lanes)),
        ],
        out_specs=pl.BlockSpec((step_tokens, ds, lanes), lambda i: (i, 0, 0)),
        out_shape=jax.ShapeDtypeStruct((n, ds, lanes), F32),
        scratch_shapes=[
            buf, buf,
            pltpu.SemaphoreType.DMA((2,)),
            pltpu.VMEM((tb, PEER_ROWS, LANES), F32),
            pltpu.VMEM((PEER_ROWS, TOKEN_BLOCK), F32),
            pltpu.VMEM((2, PEER_ROWS, TOKEN_BLOCK), F32),
            pltpu.VMEM((TOKEN_BLOCK, PEER_ROWS), I32),
            pltpu.SMEM((TOKEN_BLOCK, PEER_ROWS), I32),
            pltpu.SemaphoreType.DMA(()),
        ],
        compiler_params=_cparams(("arbitrary",)),
        name="peer_experts",
    )(q3, q3, keys_bf, tab3, xw3, h3, gf3)


def _row_tile(n, pref):
    t = pref
    while n % t:
        t //= 2
    return t


def kernel(x, meta_tokens, norm_mix_g, w_in, b_in, sinks, conv_w, conv_b, w_r, b_r, w_i, b_i, lru_lambda, gn_attn_g, gn_lru_g, w_out, norm_ffn_g, peer_wq, peer_sub_keys, peer_u, peer_v, final_norm_g):
    n_batch, seq, d = x.shape
    n = n_batch * seq
    assert w_in.shape[0] == 1 and seq % ATT_BLOCK == 0
    x2 = x.reshape(n, d)
    row = lambda a: a.reshape(1, -1)

    w_in_bf = w_in[0].astype(BF16)
    g_mix = row(norm_mix_g[0])
    b_in_r = row(b_in[0])
    qkv, xb, gate = _input_proj(x2, g_mix, w_in_bf, b_in_r, _row_tile(n, 256))
    qkv_m, xb_m, gate_m = _input_proj(meta_tokens.astype(F32), g_mix, w_in_bf, b_in_r, N_META)

    km = qkv_m[:, Q_DIM:Q_DIM + KV_DIM]
    vm = qkv_m[:, Q_DIM + KV_DIM:]
    attn_n = _attention(qkv, km, vm, sinks[0].astype(F32), row(gn_attn_g[0]), n_batch)

    lru_args = (conv_w[0], row(conv_b[0]), _block_diag_groups(w_r[0]).astype(BF16), row(b_r[0]),
                _block_diag_groups(w_i[0]).astype(BF16), row(b_i[0]), row(lru_lambda[0]),
                row(gn_lru_g[0]))
    zeros8 = jnp.zeros((8, LRU_WIDTH), F32)
    _, h_meta = _lru(xb_m, gate_m, *lru_args, zeros8, zeros8, 1, N_META)
    lru_n, _ = _lru(xb, gate, *lru_args, h_meta, xb_m[N_META - 8:], n_batch, _row_tile(seq, 256))

    w_out_bf = w_out[0].astype(BF16)
    h1, xw, q = _mix(attn_n, lru_n, x2, w_out_bf[:Q_DIM], w_out_bf[Q_DIM:], row(norm_ffn_g[0]),
                     peer_wq[0].astype(BF16), _row_tile(n, 256))

    ne = peer_u.shape[1]
    ds = d // LANES
    tab3 = _cast_tables(peer_u[0], peer_v[0], 256).reshape(ne, 2 * ds, LANES)
    y3 = _experts(q, peer_sub_keys[0].astype(BF16), tab3, xw.reshape(n, ds, LANES),
                  h1.reshape(n, ds, LANES), final_norm_g.reshape(ds, LANES), 8)
    return y3.reshape(n_batch, seq, d)
```

```python
import functools

import numpy as np
import jax
import jax.numpy as jnp
from jax import lax
from jax.experimental import pallas as pl
from jax.experimental.pallas import tpu as pltpu

F32 = jnp.float32
BF16 = jnp.bfloat16
I32 = jnp.int32

EPS = 1e-6
N_META = 16
HEAD_DIM = 64
N_Q_HEADS = 16
N_KV_HEADS = 2
GROUP = N_Q_HEADS // N_KV_HEADS
Q_DIM = N_Q_HEADS * HEAD_DIM
KV_DIM = N_KV_HEADS * HEAD_DIM
ATT_BLOCK = 128
LRU_WIDTH = 1024
LRU_BLOCKS = 16
LRU_BW = LRU_WIDTH // LRU_BLOCKS
LRU_C = 8.0
GATE_GROUP = 256
PEER_HEADS = 8
N_KEYS = 128
PEER_TOPK = 16
PEER_ROWS = PEER_HEADS * PEER_TOPK
NEG_BIG = -1e30

VMEM_LIMIT = 56 * 1024 * 1024


def _cparams(sem, vmem=VMEM_LIMIT):
    return pltpu.CompilerParams(dimension_semantics=sem, vmem_limit_bytes=vmem)


def _const_spec(shape):
    nd = len(shape)
    return pl.BlockSpec(shape, lambda *_: (0,) * nd, pipeline_mode=pl.Buffered(1))


def _rms(x, g):
    return x * lax.rsqrt(jnp.mean(x * x, axis=-1, keepdims=True) + EPS) * g


def _gelu(x):
    return jax.nn.gelu(x, approximate=True)


def _proj_kernel(x_ref, g_ref, w_ref, b_ref, qkv_ref, xb_ref, gate_ref):
    hn = _rms(x_ref[...], g_ref[...]).astype(BF16)
    n_qkv = Q_DIM + 2 * KV_DIM

    def proj(lo, hi):
        return jnp.dot(hn, w_ref[:, lo:hi], preferred_element_type=F32) + b_ref[:, lo:hi]

    step = 256
    for lo in range(0, n_qkv, step):
        qkv_ref[:, lo:lo + step] = proj(lo, lo + step).astype(BF16)
    for lo in range(0, LRU_WIDTH, step):
        xb_ref[:, lo:lo + step] = proj(n_qkv + lo, n_qkv + lo + step)
        gate_ref[:, lo:lo + step] = proj(n_qkv + LRU_WIDTH + lo, n_qkv + LRU_WIDTH + lo + step)


def _input_proj(x2, g, w_bf, b, tm):
    n, d = x2.shape
    n_qkv = Q_DIM + 2 * KV_DIM
    in_dim = w_bf.shape[1]
    return pl.pallas_call(
        _proj_kernel,
        grid=(n // tm,),
        in_specs=[
            pl.BlockSpec((tm, d), lambda i: (i, 0)),
            _const_spec((1, d)),
            _const_spec((d, in_dim)),
            _const_spec((1, in_dim)),
        ],
        out_specs=[
            pl.BlockSpec((tm, n_qkv), lambda i: (i, 0)),
            pl.BlockSpec((tm, LRU_WIDTH), lambda i: (i, 0)),
            pl.BlockSpec((tm, LRU_WIDTH), lambda i: (i, 0)),
        ],
        out_shape=[
            jax.ShapeDtypeStruct((n, n_qkv), BF16),
            jax.ShapeDtypeStruct((n, LRU_WIDTH), F32),
            jax.ShapeDtypeStruct((n, LRU_WIDTH), F32),
        ],
        compiler_params=_cparams(("parallel",)),
        name="input_proj",
    )(x2, g, w_bf, b)


def _attn_kernel(nb, q_ref, kc_ref, kp_ref, vc_ref, vp_ref, km_ref, vm_ref, sink_ref, g_ref,
                 o_ref, acc_ref):
    blk = pl.program_id(0) % nb
    row = lax.broadcasted_iota(I32, (ATT_BLOCK, 2 * ATT_BLOCK), 0)
    col = lax.broadcasted_iota(I32, (ATT_BLOCK, 2 * ATT_BLOCK), 1)
    no_prev = jnp.where(blk > 0, 0, 2 * ATT_BLOCK)
    in_prev = jnp.logical_and(col < ATT_BLOCK, col > row + no_prev)
    in_cur = jnp.logical_and(col >= ATT_BLOCK, (col - ATT_BLOCK) <= row)
    band = jnp.logical_or(in_prev, in_cur)
    scale = HEAD_DIM ** -0.5
    nt = (((1,), (1,)), ((), ()))
    for g in range(N_KV_HEADS):
        ks = slice(g * HEAD_DIM, (g + 1) * HEAD_DIM)
        kb = jnp.concatenate([kp_ref[:, ks], kc_ref[:, ks]], axis=0)
        vb = jnp.concatenate([vp_ref[:, ks], vc_ref[:, ks]], axis=0)
        kmg = km_ref[:, ks]
        vmg = vm_ref[:, ks]
        for hh in range(GROUP):
            h = g * GROUP + hh
            qh = q_ref[:, h * HEAD_DIM:(h + 1) * HEAD_DIM]
            sb = lax.dot_general(qh, kb, nt, preferred_element_type=F32) * scale
            sm = lax.dot_general(qh, kmg, nt, preferred_element_type=F32) * scale
            sb = jnp.where(band, sb, NEG_BIG)
            sink = sink_ref[h]
            m = jnp.maximum(jnp.max(sb, axis=-1, keepdims=True), jnp.max(sm, axis=-1, keepdims=True))
            m = jnp.maximum(m, sink)
            pb = jnp.exp(sb - m)
            pm = jnp.exp(sm - m)
            den = (jnp.sum(pb, axis=-1, keepdims=True) + jnp.sum(pm, axis=-1, keepdims=True)
                   + jnp.exp(sink - m))
            o = (jnp.dot(pb.astype(BF16), vb, preferred_element_type=F32)
                 + jnp.dot(pm.astype(BF16), vmg, preferred_element_type=F32))
            acc_ref[:, h * HEAD_DIM:(h + 1) * HEAD_DIM] = o / den
    o_ref[...] = _rms(acc_ref[...], g_ref[...]).astype(BF16)


def _attention(qkv, km, vm, sinks, g, n_batch):
    n = qkv.shape[0]
    nblk = n // ATT_BLOCK
    nb = nblk // n_batch
    kcol = Q_DIM // KV_DIM
    return pl.pallas_call(
        functools.partial(_attn_kernel, nb),
        grid=(nblk,),
        in_specs=[
            pl.BlockSpec((ATT_BLOCK, Q_DIM), lambda i: (i, 0)),
            pl.BlockSpec((ATT_BLOCK, KV_DIM), lambda i: (i, kcol)),
            pl.BlockSpec((ATT_BLOCK, KV_DIM), lambda i: (jnp.maximum(i - 1, 0), kcol)),
            pl.BlockSpec((ATT_BLOCK, KV_DIM), lambda i: (i, kcol + 1)),
            pl.BlockSpec((ATT_BLOCK, KV_DIM), lambda i: (jnp.maximum(i - 1, 0), kcol + 1)),
            _const_spec((N_META, KV_DIM)),
            _const_spec((N_META, KV_DIM)),
            pl.BlockSpec(memory_space=pltpu.SMEM),
            _const_spec((1, Q_DIM)),
        ],
        out_specs=pl.BlockSpec((ATT_BLOCK, Q_DIM), lambda i: (i, 0)),
        out_shape=jax.ShapeDtypeStruct((n, Q_DIM), BF16),
        scratch_shapes=[pltpu.VMEM((ATT_BLOCK, Q_DIM), F32)],
        compiler_params=_cparams(("parallel",)),
        name="swa_attention",
    )(qkv, qkv, qkv, qkv, qkv, km, vm, sinks, g)


def _log_sigmoid(x):
    return -(jnp.maximum(-x, 0.0) + jnp.log1p(jnp.exp(-jnp.abs(x))))


def _lru_kernel(tt, x_ref, gate_ref, cw_ref, cb_ref, wr_ref, br_ref, wi_ref, bi_ref, lam_ref, g_ref,
                h0_ref, tail0_ref, o_ref, hlast_ref, xfull_ref, h_ref):
    t = pl.program_id(1)

    @pl.when(t == 0)
    def _():
        h_ref[...] = h0_ref[...]
        xfull_ref[0:8, :] = tail0_ref[...]

    x = x_ref[...]
    xfull_ref[8:8 + tt, :] = x
    xc = (cb_ref[...]
          + cw_ref[0:1, :] * xfull_ref[5:5 + tt, :]
          + cw_ref[1:2, :] * xfull_ref[6:6 + tt, :]
          + cw_ref[2:3, :] * xfull_ref[7:7 + tt, :]
          + cw_ref[3:4, :] * x)
    xfull_ref[0:8, :] = x[tt - 8:tt, :]

    xcb = xc.astype(BF16)
    r_parts, i_parts = [], []
    for gi in range(LRU_WIDTH // GATE_GROUP):
        sl = slice(gi * GATE_GROUP, (gi + 1) * GATE_GROUP)
        r_parts.append(jnp.dot(xcb[:, sl], wr_ref[gi], preferred_element_type=F32))
        i_parts.append(jnp.dot(xcb[:, sl], wi_ref[gi], preferred_element_type=F32))
    r = jax.nn.sigmoid(jnp.concatenate(r_parts, axis=-1) + br_ref[...])
    ig = jax.nn.sigmoid(jnp.concatenate(i_parts, axis=-1) + bi_ref[...])
    log_a = LRU_C * r * _log_sigmoid(lam_ref[...])
    a = jnp.exp(log_a)
    th = jnp.tanh(log_a)
    b = jnp.sqrt(-2.0 * th / (1.0 - th)) * (ig * xc)

    rows = lax.broadcasted_iota(I32, a.shape, 0)
    s = 1
    while s < tt:
        live = rows >= s
        a_sh = pltpu.roll(a, s, 0)
        b_sh = pltpu.roll(b, s, 0)
        b = jnp.where(live, a * b_sh + b, b)
        a = jnp.where(live, a * a_sh, a)
        s *= 2
    h = b + a * h_ref[0:1, :]
    hl = h[tt - 1:tt, :]
    h_ref[...] = jnp.broadcast_to(hl, h_ref.shape)
    hlast_ref[...] = jnp.broadcast_to(hl, hlast_ref.shape)

    o_ref[...] = _rms(_gelu(gate_ref[...]) * h, g_ref[...]).astype(BF16)


def _lru(xb, gate, cw, cb, wr_bd, br, wi_bd, bi, lam, g, h0, tail0, n_batch, tt):
    n = xb.shape[0]
    nt = n // n_batch // tt
    w = LRU_WIDTH
    ng = w // GATE_GROUP
    row_spec = pl.BlockSpec((tt, w), lambda b, t: (b * nt + t, 0))
    return pl.pallas_call(
        functools.partial(_lru_kernel, tt),
        grid=(n_batch, nt),
        in_specs=[
            row_spec, row_spec,
            _const_spec((4, w)), _const_spec((1, w)),
            _const_spec((ng, GATE_GROUP, GATE_GROUP)), _const_spec((1, w)),
            _const_spec((ng, GATE_GROUP, GATE_GROUP)), _const_spec((1, w)),
            _const_spec((1, w)), _const_spec((1, w)),
            _const_spec((8, w)), _const_spec((8, w)),
        ],
        out_specs=[row_spec, pl.BlockSpec((8, w), lambda b, t: (b * nt + t, 0))],
        out_shape=[jax.ShapeDtypeStruct((n, w), BF16),
                   jax.ShapeDtypeStruct((n_batch * nt * 8, w), F32)],
        scratch_shapes=[pltpu.VMEM((tt + 8, w), F32), pltpu.VMEM((8, w), F32)],
        compiler_params=_cparams(("arbitrary", "arbitrary")),
        name="rg_lru",
    )(xb, gate, cw, cb, wr_bd, br, wi_bd, bi, lam, g, h0, tail0)


def _block_diag_groups(w):
    per = GATE_GROUP // LRU_BW
    w4 = w.reshape(LRU_BLOCKS // per, per, LRU_BW, LRU_BW)
    eye = jnp.eye(per, dtype=w.dtype)
    return jnp.einsum('gacd,ab->gacbd', w4, eye).reshape(LRU_BLOCKS // per, GATE_GROUP, GATE_GROUP)


def _mix_kernel(a_ref, l_ref, x_ref, wa_ref, wl_ref, g_ref, wq_ref, h_ref, xw_ref, q_ref):
    h = (x_ref[...]
         + jnp.dot(a_ref[...], wa_ref[...], preferred_element_type=F32)
         + jnp.dot(l_ref[...], wl_ref[...], preferred_element_type=F32))
    h_ref[...] = h
    hn = _rms(h, g_ref[...]).astype(BF16)
    xw_ref[...] = hn
    q = jnp.dot(hn, wq_ref[...], preferred_element_type=F32).astype(BF16)
    dk = q_ref.shape[2]
    for hd in range(q_ref.shape[0]):
        q_ref[hd] = q[:, hd * dk:(hd + 1) * dk]


def _mix(attn_n, lru_n, x2, wa, wl, g, wq, tm):
    n, d = x2.shape
    dq = wq.shape[1]
    return pl.pallas_call(
        _mix_kernel,
        grid=(n // tm,),
        in_specs=[
            pl.BlockSpec((tm, Q_DIM), lambda i: (i, 0)),
            pl.BlockSpec((tm, LRU_WIDTH), lambda i: (i, 0)),
            pl.BlockSpec((tm, d), lambda i: (i, 0)),
            _const_spec((Q_DIM, d)), _const_spec((LRU_WIDTH, d)), _const_spec((1, d)),
            _const_spec((d, dq)),
        ],
        out_specs=[
            pl.BlockSpec((tm, d), lambda i: (i, 0)),
            pl.BlockSpec((tm, d), lambda i: (i, 0)),
            pl.BlockSpec((PEER_HEADS, tm, dq // PEER_HEADS), lambda i: (0, i, 0)),
        ],
        out_shape=[
            jax.ShapeDtypeStruct((n, d), F32),
            jax.ShapeDtypeStruct((n, d), BF16),
            jax.ShapeDtypeStruct((PEER_HEADS, n, dq // PEER_HEADS), BF16),
        ],
        compiler_params=_cparams(("parallel",)),
        name="out_proj_peer_query",
    )(attn_n, lru_n, x2, wa, wl, g, wq)


def _top16_rows(s, ids):
    vals, picks = [], []
    for _ in range(PEER_TOPK):
        m = jnp.max(s, axis=0, keepdims=True)
        pick = jnp.min(jnp.where(s == m, ids, jnp.inf), axis=0, keepdims=True)
        s = jnp.where(ids == pick, -jnp.inf, s)
        vals.append(m)
        picks.append(pick)
    return jnp.concatenate(vals, axis=0), jnp.concatenate(picks, axis=0)


def _peer_select(qh, keys):
    tm = qh.shape[0]
    nt = (((1,), (1,)), ((), ()))
    key_ids = lax.broadcasted_iota(I32, (N_KEYS, tm), 0).astype(F32)
    tops = []
    for p in range(2):
        qp = qh[:, p * N_KEYS:(p + 1) * N_KEYS]
        s = lax.dot_general(keys[p], qp, nt, preferred_element_type=F32)
        tops.append(_top16_rows(s, key_ids))
    (v0, i0), (v1, i1) = tops
    sub8 = lax.broadcasted_iota(I32, (8, tm), 0).astype(F32)
    cand = [v0[0:1] + v1]
    cid = [lax.broadcasted_iota(I32, (PEER_TOPK, tm), 0).astype(F32)]
    eid = [i0[0:1] * N_KEYS + i1]
    for k1 in range(1, PEER_TOPK):
        cand.append(v0[k1:k1 + 1] + v1[0:8])
        cid.append(sub8 + float(k1 * PEER_TOPK))
        eid.append(i0[k1:k1 + 1] * N_KEYS + i1[0:8])
    cand = jnp.concatenate(cand, axis=0)
    cid = jnp.concatenate(cid, axis=0)
    eid = jnp.concatenate(eid, axis=0)
    best, experts = [], []
    for _ in range(PEER_TOPK):
        m = jnp.max(cand, axis=0, keepdims=True)
        pick = jnp.min(jnp.where(cand == m, cid, jnp.inf), axis=0, keepdims=True)
        sel = cid == pick
        experts.append(jnp.max(jnp.where(sel, eid, -1.0), axis=0, keepdims=True))
        cand = jnp.where(sel, -jnp.inf, cand)
        best.append(m)
    best = jnp.concatenate(best, axis=0)
    ex = jnp.exp(best - best[0:1])
    return jnp.concatenate(experts, axis=0), ex / jnp.sum(ex, axis=0, keepdims=True)


def _expert_table(u, v):
    ne, d = u.shape
    return jnp.concatenate([u.astype(BF16), v.astype(BF16)], axis=1).reshape(ne, 2 * d // 128, 128)


SUBLANES = 8
LANES = 128


def _fold_pair(x, y, sh, rows):
    x2 = x + pltpu.roll(x, sh, 0)
    y2 = y + pltpu.roll(y, SUBLANES - sh, 0)
    return jnp.where((rows // sh) % 2 == 1, x2, y2)


def _fold_order():
    rows = np.arange(SUBLANES)[:, None]

    def fold(x, y, sh):
        x2 = x + np.roll(x, sh, 0)
        y2 = y + np.roll(y, SUBLANES - sh, 0)
        return np.where((rows // sh) % 2 == 1, x2, y2)

    vals = []
    for i in range(SUBLANES):
        v = np.zeros((SUBLANES, SUBLANES * SUBLANES), np.int64)
        v[np.arange(SUBLANES), i * SUBLANES + np.arange(SUBLANES)] = 1
        vals.append(v)
    sh = SUBLANES // 2
    while len(vals) > 1:
        vals = [fold(vals[2 * k], vals[2 * k + 1], sh) for k in range(len(vals) // 2)]
        sh //= 2
    out = vals[0].reshape(SUBLANES, SUBLANES, SUBLANES)
    src_of_row = [int(np.argmax(out[r].sum(-1))) for r in range(SUBLANES)]
    for r in range(SUBLANES):
        assert (out[r, src_of_row[r]] == 1).all() and out[r].sum() == SUBLANES
    order = [0] * SUBLANES
    for r, p in enumerate(src_of_row):
        order[p] = r
    return tuple(order)


_FOLD_ORDER = _fold_order()


def _sublane_sums(parts, rows):
    vals = [parts[r] for r in _FOLD_ORDER]
    sh = SUBLANES // 2
    while len(vals) > 1:
        vals = [_fold_pair(vals[2 * k], vals[2 * k + 1], sh, rows) for k in range(len(vals) // 2)]
        sh //= 2
    return vals[0]


TOKEN_BLOCK = 128


def _expert_kernel(tb, q0_ref, qn_ref, keys_ref, tab_ref, xw_ref, h_ref, gf_ref,
                   y_ref, buf_a_ref, buf_b_ref, sem_ref, wrep_ref, e_ref, g_ref, idxv_ref, idx_ref,
                   isem_ref):
    i = pl.program_id(0)
    nsteps = pl.num_programs(0)
    step_tokens = 2 * tb
    steps_per_blk = TOKEN_BLOCK // step_tokens
    assert steps_per_blk == PEER_HEADS
    blk = i // steps_per_blk
    sub = i % steps_per_blk
    nblk = nsteps // steps_per_blk
    rows_per_half = tb * PEER_ROWS
    ds = xw_ref.shape[1]
    bufs = (buf_a_ref, buf_b_ref)
    tail = TOKEN_BLOCK - step_tokens

    def row_copy(e, half, r):
        return pltpu.make_async_copy(tab_ref.at[e], bufs[half].at[r], sem_ref.at[half])

    def half_wait(half):
        pltpu.make_async_copy(tab_ref.at[pl.ds(0, rows_per_half)], bufs[half], sem_ref.at[half]).wait()

    def select_head(q_ref, hd, slot):
        experts, gates = _peer_select(q_ref[hd], keys_ref[hd])
        r0 = pl.multiple_of(hd * PEER_TOPK, PEER_TOPK)
        e_ref[pl.ds(r0, PEER_TOPK), :] = experts
        g_ref[slot, pl.ds(r0, PEER_TOPK), :] = gates

    def ids_copy(lo, hi):
        return pltpu.make_async_copy(idxv_ref.at[pl.ds(lo, hi - lo)], idx_ref.at[pl.ds(lo, hi - lo)], isem_ref)

    @pl.when(i == 0)
    def _():
        def head_body(hd, carry):
            select_head(q0_ref, hd, 0)
            return carry
        lax.fori_loop(0, PEER_HEADS, head_body, 0)
        idxv_ref[...] = jnp.transpose(e_ref[...]).astype(I32)
        ids_copy(0, TOKEN_BLOCK).start()
        ids_copy(0, TOKEN_BLOCK).wait()

        def body(t, carry):
            for j in range(PEER_ROWS):
                row_copy(idx_ref[t, j], 0, t * PEER_ROWS + j).start()
            return carry
        lax.fori_loop(0, tb, body, 0)

    late_ids = jnp.logical_and(sub == 0, blk > 0)
    early_ids = jnp.logical_and(sub == steps_per_blk - 1, blk + 1 < nblk)

    @pl.when(late_ids)
    def _():
        ids_copy(tail, TOKEN_BLOCK).start()

    @pl.when(blk + 1 < nblk)
    def _():
        select_head(qn_ref, sub, (blk + 1) % 2)

    @pl.when(early_ids)
    def _():
        idxv_ref[...] = jnp.transpose(e_ref[...]).astype(I32)
        ids_copy(0, tail).start()

    rows8 = lax.broadcasted_iota(I32, (SUBLANES, LANES), 0)
    lane = lax.broadcasted_iota(I32, (PEER_ROWS, TOKEN_BLOCK), 1)
    half_rows = PEER_ROWS // 2
    gslot = blk % 2

    def process(half, next_row0):
        other = 1 - half
        t0 = half * tb
        half_wait(half)
        buf = bufs[half]
        base = sub * step_tokens + t0

        def prefetch(t, j):
            row_copy(idx_ref[next_row0 + t, j], other, t * PEER_ROWS + j).start(priority=j % 2)

        pre = jnp.zeros((PEER_ROWS, TOKEN_BLOCK), F32)
        for t in range(tb):
            x_t = xw_ref[t0 + t]
            sums = []
            for g0 in range(0, PEER_ROWS, SUBLANES):
                for j in range(g0 // 2, (g0 + SUBLANES) // 2):
                    prefetch(t, j)
                parts = []
                for j in range(g0, g0 + SUBLANES):
                    p = (buf[t * PEER_ROWS + j, 0:ds, :] * x_t).astype(F32)
                    parts.append(p[0:SUBLANES] + p[SUBLANES:2 * SUBLANES])
                sums.append(_sublane_sums(parts, rows8))
            rowsum = jnp.sum(jnp.concatenate(sums, axis=0), axis=-1, keepdims=True)
            pre = jnp.where(lane == base + t, rowsum, pre)
        w = g_ref[gslot] * _gelu(pre)
        for t in range(tb):
            wcol = jnp.sum(jnp.where(lane == base + t, w, 0.0), axis=-1, keepdims=True)
            wrep_ref[t] = jnp.broadcast_to(wcol, (PEER_ROWS, LANES))
        n_acc = 4
        for t in range(tb):
            accs = [jnp.zeros((ds, LANES), F32) for _ in range(n_acc)]
            for j in range(PEER_ROWS):
                if j % 2 == 0:
                    prefetch(t, half_rows + j // 2)
                v = buf[t * PEER_ROWS + j, ds:2 * ds, :].astype(F32)
                accs[j % n_acc] = accs[j % n_acc] + v * wrep_ref[t, j:j + 1, :]
            hp = h_ref[t0 + t] + ((accs[0] + accs[1]) + (accs[2] + accs[3]))
            ms = jnp.mean(jnp.mean(hp * hp, axis=-1, keepdims=True), axis=0, keepdims=True)
            y_ref[t0 + t] = hp * lax.rsqrt(ms + EPS) * gf_ref[...]

    process(0, sub * step_tokens + tb)

    @pl.when(early_ids)
    def _():
        ids_copy(0, tail).wait()

    process(1, ((sub + 1) % steps_per_blk) * step_tokens)

    @pl.when(late_ids)
    def _():
        ids_copy(tail, TOKEN_BLOCK).wait()

    @pl.when(i == nsteps - 1)
    def _():
        half_wait(0)


def _experts(q3, keys_bf, tab3, xw3, h3, gf3, tb):
    n, ds, lanes = h3.shape
    assert n % TOKEN_BLOCK == 0
    step_tokens = 2 * tb
    nsteps = n // step_tokens
    steps_per_blk = TOKEN_BLOCK // step_tokens
    nblk = n // TOKEN_BLOCK
    dk = q3.shape[2]
    buf = pltpu.VMEM((tb * PEER_ROWS, 2 * ds, lanes), BF16)
    return pl.pallas_call(
        functools.partial(_expert_kernel, tb),
        grid=(nsteps,),
        in_specs=[
            pl.BlockSpec((PEER_HEADS, TOKEN_BLOCK, dk), lambda i: (0, 0, 0)),
            pl.BlockSpec((PEER_HEADS, TOKEN_BLOCK, dk),
                         lambda i: (0, jnp.minimum(i // steps_per_blk + 1, nblk - 1), 0)),
            _const_spec((PEER_HEADS, 2, N_KEYS, N_KEYS)),
            pl.BlockSpec(memory_space=pl.ANY),
            pl.BlockSpec((step_tokens, ds, lanes), lambda i: (i, 0, 0)),
            pl.BlockSpec((step_tokens, ds, lanes), lambda i: (i, 0, 0)),
            _const_spec((ds, lanes)),
        ],
        out_specs=pl.BlockSpec((step_tokens, ds, lanes), lambda i: (i, 0, 0)),
        out_shape=jax.ShapeDtypeStruct((n, ds, lanes), F32),
        scratch_shapes=[
            buf, buf,
            pltpu.SemaphoreType.DMA((2,)),
            pltpu.VMEM((tb, PEER_ROWS, LANES), F32),
            pltpu.VMEM((PEER_ROWS, TOKEN_BLOCK), F32),
            pltpu.VMEM((2, PEER_ROWS, TOKEN_BLOCK), F32),
            pltpu.VMEM((TOKEN_BLOCK, PEER_ROWS), I32),
            pltpu.SMEM((TOKEN_BLOCK, PEER_ROWS), I32),
            pltpu.SemaphoreType.DMA(()),
        ],
        compiler_params=_cparams(("arbitrary",)),
        name="peer_experts",
    )(q3, q3, keys_bf, tab3, xw3, h3, gf3)


def _row_tile(n, pref):
    t = pref
    while n % t:
        t //= 2
    return t


def kernel(x, meta_tokens, norm_mix_g, w_in, b_in, sinks, conv_w, conv_b, w_r, b_r, w_i, b_i, lru_lambda, gn_attn_g, gn_lru_g, w_out, norm_ffn_g, peer_wq, peer_sub_keys, peer_u, peer_v, final_norm_g):
    n_batch, seq, d = x.shape
    n = n_batch * seq
    assert w_in.shape[0] == 1 and seq % ATT_BLOCK == 0
    x2 = x.reshape(n, d)
    row = lambda a: a.reshape(1, -1)

    w_in_bf = w_in[0].astype(BF16)
    g_mix = row(norm_mix_g[0])
    b_in_r = row(b_in[0])
    qkv, xb, gate = _input_proj(x2, g_mix, w_in_bf, b_in_r, _row_tile(n, 256))
    qkv_m, xb_m, gate_m = _input_proj(meta_tokens.astype(F32), g_mix, w_in_bf, b_in_r, N_META)

    km = qkv_m[:, Q_DIM:Q_DIM + KV_DIM]
    vm = qkv_m[:, Q_DIM + KV_DIM:]
    attn_n = _attention(qkv, km, vm, sinks[0].astype(F32), row(gn_attn_g[0]), n_batch)

    lru_args = (conv_w[0], row(conv_b[0]), _block_diag_groups(w_r[0]).astype(BF16), row(b_r[0]),
                _block_diag_groups(w_i[0]).astype(BF16), row(b_i[0]), row(lru_lambda[0]),
                row(gn_lru_g[0]))
    zeros8 = jnp.zeros((8, LRU_WIDTH), F32)
    _, h_meta = _lru(xb_m, gate_m, *lru_args, zeros8, zeros8, 1, N_META)
    lru_n, _ = _lru(xb, gate, *lru_args, h_meta, xb_m[N_META - 8:], n_batch, _row_tile(seq, 256))

    w_out_bf = w_out[0].astype(BF16)
    h1, xw, q = _mix(attn_n, lru_n, x2, w_out_bf[:Q_DIM], w_out_bf[Q_DIM:], row(norm_ffn_g[0]),
                     peer_wq[0].astype(BF16), _row_tile(n, 256))

    ds = d // LANES
    tab3 = _expert_table(peer_u[0], peer_v[0])
    y3 = _experts(q, peer_sub_keys[0].astype(BF16), tab3, xw.reshape(n, ds, LANES),
                  h1.reshape(n, ds, LANES), final_norm_g.reshape(ds, LANES), 8)
    return y3.reshape(n_batch, seq, d)
```

```python
import functools

import numpy as np
import jax
import jax.numpy as jnp
from jax import lax
from jax.experimental import pallas as pl
from jax.experimental.pallas import tpu as pltpu

F32 = jnp.float32
BF16 = jnp.bfloat16
I32 = jnp.int32

EPS = 1e-6
N_META = 16
HEAD_DIM = 64
N_Q_HEADS = 16
N_KV_HEADS = 2
GROUP = N_Q_HEADS // N_KV_HEADS
Q_DIM = N_Q_HEADS * HEAD_DIM
KV_DIM = N_KV_HEADS * HEAD_DIM
ATT_BLOCK = 128
LRU_WIDTH = 1024
LRU_BLOCKS = 16
LRU_BW = LRU_WIDTH // LRU_BLOCKS
LRU_C = 8.0
GATE_GROUP = 256
PEER_HEADS = 8
N_KEYS = 128
PEER_TOPK = 16
PEER_ROWS = PEER_HEADS * PEER_TOPK
NEG_BIG = -1e30

VMEM_LIMIT = 56 * 1024 * 1024


def _cparams(sem, vmem=VMEM_LIMIT):
    return pltpu.CompilerParams(dimension_semantics=sem, vmem_limit_bytes=vmem)


def _const_spec(shape):
    nd = len(shape)
    return pl.BlockSpec(shape, lambda *_: (0,) * nd, pipeline_mode=pl.Buffered(1))


def _rms(x, g):
    return x * lax.rsqrt(jnp.mean(x * x, axis=-1, keepdims=True) + EPS) * g


def _gelu(x):
    return jax.nn.gelu(x, approximate=True)


def _proj_kernel(x_ref, g_ref, w_ref, b_ref, qkv_ref, xb_ref, gate_ref):
    hn = _rms(x_ref[...], g_ref[...]).astype(BF16)
    n_qkv = Q_DIM + 2 * KV_DIM

    def proj(lo, hi):
        return jnp.dot(hn, w_ref[:, lo:hi], preferred_element_type=F32) + b_ref[:, lo:hi]

    step = 256
    for lo in range(0, n_qkv, step):
        qkv_ref[:, lo:lo + step] = proj(lo, lo + step).astype(BF16)
    for lo in range(0, LRU_WIDTH, step):
        xb_ref[:, lo:lo + step] = proj(n_qkv + lo, n_qkv + lo + step)
        gate_ref[:, lo:lo + step] = proj(n_qkv + LRU_WIDTH + lo, n_qkv + LRU_WIDTH + lo + step)


def _input_proj(x2, g, w_bf, b, tm):
    n, d = x2.shape
    n_qkv = Q_DIM + 2 * KV_DIM
    in_dim = w_bf.shape[1]
    return pl.pallas_call(
        _proj_kernel,
        grid=(n // tm,),
        in_specs=[
            pl.BlockSpec((tm, d), lambda i: (i, 0)),
            _const_spec((1, d)),
            _const_spec((d, in_dim)),
            _const_spec((1, in_dim)),
        ],
        out_specs=[
            pl.BlockSpec((tm, n_qkv), lambda i: (i, 0)),
            pl.BlockSpec((tm, LRU_WIDTH), lambda i: (i, 0)),
            pl.BlockSpec((tm, LRU_WIDTH), lambda i: (i, 0)),
        ],
        out_shape=[
            jax.ShapeDtypeStruct((n, n_qkv), BF16),
            jax.ShapeDtypeStruct((n, LRU_WIDTH), F32),
            jax.ShapeDtypeStruct((n, LRU_WIDTH), F32),
        ],
        compiler_params=_cparams(("parallel",)),
        name="input_proj",
    )(x2, g, w_bf, b)


def _attn_kernel(nb, q_ref, kc_ref, kp_ref, vc_ref, vp_ref, km_ref, vm_ref, sink_ref, g_ref,
                 o_ref, acc_ref):
    blk = pl.program_id(0) % nb
    row = lax.broadcasted_iota(I32, (ATT_BLOCK, 2 * ATT_BLOCK), 0)
    col = lax.broadcasted_iota(I32, (ATT_BLOCK, 2 * ATT_BLOCK), 1)
    no_prev = jnp.where(blk > 0, 0, 2 * ATT_BLOCK)
    in_prev = jnp.logical_and(col < ATT_BLOCK, col > row + no_prev)
    in_cur = jnp.logical_and(col >= ATT_BLOCK, (col - ATT_BLOCK) <= row)
    band = jnp.logical_or(in_prev, in_cur)
    scale = HEAD_DIM ** -0.5
    nt = (((1,), (1,)), ((), ()))
    kvs = []
    for g in range(N_KV_HEADS):
        ks = slice(g * HEAD_DIM, (g + 1) * HEAD_DIM)
        kvs.append((jnp.concatenate([kp_ref[:, ks], kc_ref[:, ks]], axis=0),
                    jnp.concatenate([vp_ref[:, ks], vc_ref[:, ks]], axis=0),
                    km_ref[:, ks], vm_ref[:, ks]))

    def scores(h):
        kb, _, kmg, _ = kvs[h // GROUP]
        qh = q_ref[:, h * HEAD_DIM:(h + 1) * HEAD_DIM]
        sb = lax.dot_general(qh, kb, nt, preferred_element_type=F32) * scale
        sm = lax.dot_general(qh, kmg, nt, preferred_element_type=F32) * scale
        return jnp.where(band, sb, NEG_BIG), sm

    nxt = scores(0)
    for h in range(N_Q_HEADS):
        sb, sm = nxt
        if h + 1 < N_Q_HEADS:
            nxt = scores(h + 1)
        _, vb, _, vmg = kvs[h // GROUP]
        sink = sink_ref[h]
        m = jnp.maximum(jnp.max(sb, axis=-1, keepdims=True), jnp.max(sm, axis=-1, keepdims=True))
        m = jnp.maximum(m, sink)
        pb = jnp.exp(sb - m)
        pm = jnp.exp(sm - m)
        den = (jnp.sum(pb, axis=-1, keepdims=True) + jnp.sum(pm, axis=-1, keepdims=True)
               + jnp.exp(sink - m))
        o = (jnp.dot(pb.astype(BF16), vb, preferred_element_type=F32)
             + jnp.dot(pm.astype(BF16), vmg, preferred_element_type=F32))
        acc_ref[:, h * HEAD_DIM:(h + 1) * HEAD_DIM] = o / den
    o_ref[...] = _rms(acc_ref[...], g_ref[...]).astype(BF16)


def _attention(qkv, km, vm, sinks, g, n_batch):
    n = qkv.shape[0]
    nblk = n // ATT_BLOCK
    nb = nblk // n_batch
    kcol = Q_DIM // KV_DIM
    return pl.pallas_call(
        functools.partial(_attn_kernel, nb),
        grid=(nblk,),
        in_specs=[
            pl.BlockSpec((ATT_BLOCK, Q_DIM), lambda i: (i, 0)),
            pl.BlockSpec((ATT_BLOCK, KV_DIM), lambda i: (i, kcol)),
            pl.BlockSpec((ATT_BLOCK, KV_DIM), lambda i: (jnp.maximum(i - 1, 0), kcol)),
            pl.BlockSpec((ATT_BLOCK, KV_DIM), lambda i: (i, kcol + 1)),
            pl.BlockSpec((ATT_BLOCK, KV_DIM), lambda i: (jnp.maximum(i - 1, 0), kcol + 1)),
            _const_spec((N_META, KV_DIM)),
            _const_spec((N_META, KV_DIM)),
            pl.BlockSpec(memory_space=pltpu.SMEM),
            _const_spec((1, Q_DIM)),
        ],
        out_specs=pl.BlockSpec((ATT_BLOCK, Q_DIM), lambda i: (i, 0)),
        out_shape=jax.ShapeDtypeStruct((n, Q_DIM), BF16),
        scratch_shapes=[pltpu.VMEM((ATT_BLOCK, Q_DIM), F32)],
        compiler_params=_cparams(("parallel",)),
        name="swa_attention",
    )(qkv, qkv, qkv, qkv, qkv, km, vm, sinks, g)


def _log_sigmoid(x):
    return -(jnp.maximum(-x, 0.0) + jnp.log1p(jnp.exp(-jnp.abs(x))))


def _lru_kernel(tt, x_ref, gate_ref, cw_ref, cb_ref, wr_ref, br_ref, wi_ref, bi_ref, lam_ref, g_ref,
                h0_ref, tail0_ref, o_ref, hlast_ref, xfull_ref, h_ref):
    t = pl.program_id(1)

    @pl.when(t == 0)
    def _():
        h_ref[...] = h0_ref[...]
        xfull_ref[0:8, :] = tail0_ref[...]

    x = x_ref[...]
    xfull_ref[8:8 + tt, :] = x
    xc = (cb_ref[...]
          + cw_ref[0:1, :] * xfull_ref[5:5 + tt, :]
          + cw_ref[1:2, :] * xfull_ref[6:6 + tt, :]
          + cw_ref[2:3, :] * xfull_ref[7:7 + tt, :]
          + cw_ref[3:4, :] * x)
    xfull_ref[0:8, :] = x[tt - 8:tt, :]

    xcb = xc.astype(BF16)
    r_parts, i_parts = [], []
    for gi in range(LRU_WIDTH // GATE_GROUP):
        sl = slice(gi * GATE_GROUP, (gi + 1) * GATE_GROUP)
        r_parts.append(jnp.dot(xcb[:, sl], wr_ref[gi], preferred_element_type=F32))
        i_parts.append(jnp.dot(xcb[:, sl], wi_ref[gi], preferred_element_type=F32))
    r = jax.nn.sigmoid(jnp.concatenate(r_parts, axis=-1) + br_ref[...])
    ig = jax.nn.sigmoid(jnp.concatenate(i_parts, axis=-1) + bi_ref[...])
    log_a = LRU_C * r * _log_sigmoid(lam_ref[...])
    a = jnp.exp(log_a)
    th = jnp.tanh(log_a)
    b = jnp.sqrt(-2.0 * th / (1.0 - th)) * (ig * xc)

    rows = lax.broadcasted_iota(I32, a.shape, 0)
    s = 1
    while s < tt:
        live = rows >= s
        a_sh = pltpu.roll(a, s, 0)
        b_sh = pltpu.roll(b, s, 0)
        b = jnp.where(live, a * b_sh + b, b)
        a = jnp.where(live, a * a_sh, a)
        s *= 2
    h = b + a * h_ref[0:1, :]
    hl = h[tt - 1:tt, :]
    h_ref[...] = jnp.broadcast_to(hl, h_ref.shape)
    hlast_ref[...] = jnp.broadcast_to(hl, hlast_ref.shape)

    o_ref[...] = _rms(_gelu(gate_ref[...]) * h, g_ref[...]).astype(BF16)


def _lru(xb, gate, cw, cb, wr_bd, br, wi_bd, bi, lam, g, h0, tail0, n_batch, tt):
    n = xb.shape[0]
    nt = n // n_batch // tt
    w = LRU_WIDTH
    ng = w // GATE_GROUP
    row_spec = pl.BlockSpec((tt, w), lambda b, t: (b * nt + t, 0))
    return pl.pallas_call(
        functools.partial(_lru_kernel, tt),
        grid=(n_batch, nt),
        in_specs=[
            row_spec, row_spec,
            _const_spec((4, w)), _const_spec((1, w)),
            _const_spec((ng, GATE_GROUP, GATE_GROUP)), _const_spec((1, w)),
            _const_spec((ng, GATE_GROUP, GATE_GROUP)), _const_spec((1, w)),
            _const_spec((1, w)), _const_spec((1, w)),
            _const_spec((8, w)), _const_spec((8, w)),
        ],
        out_specs=[row_spec, pl.BlockSpec((8, w), lambda b, t: (b * nt + t, 0))],
        out_shape=[jax.ShapeDtypeStruct((n, w), BF16),
                   jax.ShapeDtypeStruct((n_batch * nt * 8, w), F32)],
        scratch_shapes=[pltpu.VMEM((tt + 8, w), F32), pltpu.VMEM((8, w), F32)],
        compiler_params=_cparams(("arbitrary", "arbitrary")),
        name="rg_lru",
    )(xb, gate, cw, cb, wr_bd, br, wi_bd, bi, lam, g, h0, tail0)


def _block_diag_groups(w):
    per = GATE_GROUP // LRU_BW
    w4 = w.reshape(LRU_BLOCKS // per, per, LRU_BW, LRU_BW)
    eye = jnp.eye(per, dtype=w.dtype)
    return jnp.einsum('gacd,ab->gacbd', w4, eye).reshape(LRU_BLOCKS // per, GATE_GROUP, GATE_GROUP)


def _mix_kernel(a_ref, l_ref, x_ref, wa_ref, wl_ref, g_ref, wq_ref, h_ref, xw_ref, q_ref):
    h = (x_ref[...]
         + jnp.dot(a_ref[...], wa_ref[...], preferred_element_type=F32)
         + jnp.dot(l_ref[...], wl_ref[...], preferred_element_type=F32))
    h_ref[...] = h
    hn = _rms(h, g_ref[...])
    xw_ref[...] = hn
    q = jnp.dot(hn.astype(BF16), wq_ref[...], preferred_element_type=F32).astype(BF16)
    dk = q_ref.shape[2]
    for hd in range(q_ref.shape[0]):
        q_ref[hd] = q[:, hd * dk:(hd + 1) * dk]


def _mix(attn_n, lru_n, x2, wa, wl, g, wq, tm):
    n, d = x2.shape
    dq = wq.shape[1]
    return pl.pallas_call(
        _mix_kernel,
        grid=(n // tm,),
        in_specs=[
            pl.BlockSpec((tm, Q_DIM), lambda i: (i, 0)),
            pl.BlockSpec((tm, LRU_WIDTH), lambda i: (i, 0)),
            pl.BlockSpec((tm, d), lambda i: (i, 0)),
            _const_spec((Q_DIM, d)), _const_spec((LRU_WIDTH, d)), _const_spec((1, d)),
            _const_spec((d, dq)),
        ],
        out_specs=[
            pl.BlockSpec((tm, d), lambda i: (i, 0)),
            pl.BlockSpec((tm, d), lambda i: (i, 0)),
            pl.BlockSpec((PEER_HEADS, tm, dq // PEER_HEADS), lambda i: (0, i, 0)),
        ],
        out_shape=[
            jax.ShapeDtypeStruct((n, d), F32),
            jax.ShapeDtypeStruct((n, d), F32),
            jax.ShapeDtypeStruct((PEER_HEADS, n, dq // PEER_HEADS), BF16),
        ],
        compiler_params=_cparams(("parallel",)),
        name="out_proj_peer_query",
    )(attn_n, lru_n, x2, wa, wl, g, wq)


def _top16_rows(s, ids):
    vals, picks = [], []
    for _ in range(PEER_TOPK):
        m = jnp.max(s, axis=0, keepdims=True)
        pick = jnp.min(jnp.where(s == m, ids, jnp.inf), axis=0, keepdims=True)
        s = jnp.where(ids == pick, -jnp.inf, s)
        vals.append(m)
        picks.append(pick)
    return jnp.concatenate(vals, axis=0), jnp.concatenate(picks, axis=0)


def _peer_select(qh, keys):
    tm = qh.shape[0]
    nt = (((1,), (1,)), ((), ()))
    key_ids = lax.broadcasted_iota(I32, (N_KEYS, tm), 0).astype(F32)
    tops = []
    for p in range(2):
        qp = qh[:, p * N_KEYS:(p + 1) * N_KEYS]
        s = lax.dot_general(keys[p], qp, nt, preferred_element_type=F32)
        tops.append(_top16_rows(s, key_ids))
    (v0, i0), (v1, i1) = tops
    sub8 = lax.broadcasted_iota(I32, (8, tm), 0).astype(F32)
    cand = [v0[0:1] + v1]
    cid = [lax.broadcasted_iota(I32, (PEER_TOPK, tm), 0).astype(F32)]
    eid = [i0[0:1] * N_KEYS + i1]
    for k1 in range(1, PEER_TOPK):
        cand.append(v0[k1:k1 + 1] + v1[0:8])
        cid.append(sub8 + float(k1 * PEER_TOPK))
        eid.append(i0[k1:k1 + 1] * N_KEYS + i1[0:8])
    cand = jnp.concatenate(cand, axis=0)
    cid = jnp.concatenate(cid, axis=0)
    eid = jnp.concatenate(eid, axis=0)
    best, experts = [], []
    for _ in range(PEER_TOPK):
        m = jnp.max(cand, axis=0, keepdims=True)
        pick = jnp.min(jnp.where(cand == m, cid, jnp.inf), axis=0, keepdims=True)
        sel = cid == pick
        experts.append(jnp.max(jnp.where(sel, eid, -1.0), axis=0, keepdims=True))
        cand = jnp.where(sel, -jnp.inf, cand)
        best.append(m)
    best = jnp.concatenate(best, axis=0)
    ex = jnp.exp(best - best[0:1])
    return jnp.concatenate(experts, axis=0), ex / jnp.sum(ex, axis=0, keepdims=True)


def _expert_table(u, v):
    ne, d = u.shape
    return jnp.concatenate([u.astype(BF16), v.astype(BF16)], axis=1).reshape(ne, 2 * d // 128, 128)


SUBLANES = 8
LANES = 128


def _fold_pair(x, y, sh, rows):
    x2 = x + pltpu.roll(x, sh, 0)
    y2 = y + pltpu.roll(y, SUBLANES - sh, 0)
    return jnp.where((rows // sh) % 2 == 1, x2, y2)


def _fold_order():
    rows = np.arange(SUBLANES)[:, None]

    def fold(x, y, sh):
        x2 = x + np.roll(x, sh, 0)
        y2 = y + np.roll(y, SUBLANES - sh, 0)
        return np.where((rows // sh) % 2 == 1, x2, y2)

    vals = []
    for i in range(SUBLANES):
        v = np.zeros((SUBLANES, SUBLANES * SUBLANES), np.int64)
        v[np.arange(SUBLANES), i * SUBLANES + np.arange(SUBLANES)] = 1
        vals.append(v)
    sh = SUBLANES // 2
    while len(vals) > 1:
        vals = [fold(vals[2 * k], vals[2 * k + 1], sh) for k in range(len(vals) // 2)]
        sh //= 2
    out = vals[0].reshape(SUBLANES, SUBLANES, SUBLANES)
    src_of_row = [int(np.argmax(out[r].sum(-1))) for r in range(SUBLANES)]
    for r in range(SUBLANES):
        assert (out[r, src_of_row[r]] == 1).all() and out[r].sum() == SUBLANES
    order = [0] * SUBLANES
    for r, p in enumerate(src_of_row):
        order[p] = r
    return tuple(order)


_FOLD_ORDER = _fold_order()


def _sublane_sums(parts, rows):
    vals = [parts[r] for r in _FOLD_ORDER]
    sh = SUBLANES // 2
    while len(vals) > 1:
        vals = [_fold_pair(vals[2 * k], vals[2 * k + 1], sh, rows) for k in range(len(vals) // 2)]
        sh //= 2
    return vals[0]


TOKEN_BLOCK = 128


def _expert_kernel(tb, q0_ref, qn_ref, keys_ref, tab_ref, xw_ref, h_ref, gf_ref,
                   y_ref, buf_a_ref, buf_b_ref, sem_ref, wrep_ref, e_ref, g_ref, idxv_ref, idx_ref,
                   isem_ref):
    i = pl.program_id(0)
    nsteps = pl.num_programs(0)
    step_tokens = 2 * tb
    steps_per_blk = TOKEN_BLOCK // step_tokens
    assert steps_per_blk == PEER_HEADS
    blk = i // steps_per_blk
    sub = i % steps_per_blk
    nblk = nsteps // steps_per_blk
    rows_per_half = tb * PEER_ROWS
    ds = xw_ref.shape[1] // LANES
    bufs = (buf_a_ref, buf_b_ref)

    def as_tile(ref, r):
        return jnp.concatenate([ref[r:r + 1, s * LANES:(s + 1) * LANES] for s in range(ds)], axis=0)

    tail = TOKEN_BLOCK - step_tokens

    def row_copy(e, half, r):
        return pltpu.make_async_copy(tab_ref.at[e], bufs[half].at[r], sem_ref.at[half])

    def half_wait(half):
        pltpu.make_async_copy(tab_ref.at[pl.ds(0, rows_per_half)], bufs[half], sem_ref.at[half]).wait()

    def select_head(q_ref, hd, slot):
        experts, gates = _peer_select(q_ref[hd], keys_ref[hd])
        r0 = pl.multiple_of(hd * PEER_TOPK, PEER_TOPK)
        e_ref[pl.ds(r0, PEER_TOPK), :] = experts
        g_ref[slot, pl.ds(r0, PEER_TOPK), :] = gates

    def ids_copy(lo, hi):
        return pltpu.make_async_copy(idxv_ref.at[pl.ds(lo, hi - lo)], idx_ref.at[pl.ds(lo, hi - lo)], isem_ref)

    @pl.when(i == 0)
    def _():
        def head_body(hd, carry):
            select_head(q0_ref, hd, 0)
            return carry
        lax.fori_loop(0, PEER_HEADS, head_body, 0)
        idxv_ref[...] = jnp.transpose(e_ref[...]).astype(I32)
        ids_copy(0, TOKEN_BLOCK).start()
        ids_copy(0, TOKEN_BLOCK).wait()

        def body(t, carry):
            for j in range(PEER_ROWS):
                row_copy(idx_ref[t, j], 0, t * PEER_ROWS + j).start()
            return carry
        lax.fori_loop(0, tb, body, 0)

    late_ids = jnp.logical_and(sub == 0, blk > 0)
    early_ids = jnp.logical_and(sub == steps_per_blk - 1, blk + 1 < nblk)

    @pl.when(late_ids)
    def _():
        ids_copy(tail, TOKEN_BLOCK).start()

    @pl.when(blk + 1 < nblk)
    def _():
        select_head(qn_ref, sub, (blk + 1) % 2)

    @pl.when(early_ids)
    def _():
        idxv_ref[...] = jnp.transpose(e_ref[...]).astype(I32)
        ids_copy(0, tail).start()

    rows8 = lax.broadcasted_iota(I32, (SUBLANES, LANES), 0)
    lane = lax.broadcasted_iota(I32, (PEER_ROWS, TOKEN_BLOCK), 1)
    half_rows = PEER_ROWS // 2
    gslot = blk % 2
    gain = as_tile(gf_ref, 0)

    def process(half, next_row0):
        other = 1 - half
        t0 = half * tb
        half_wait(half)
        buf = bufs[half]
        base = sub * step_tokens + t0

        def prefetch(t, j):
            row_copy(idx_ref[next_row0 + t, j], other, t * PEER_ROWS + j).start(priority=j % 2)

        pre = jnp.zeros((PEER_ROWS, TOKEN_BLOCK), F32)
        for t in range(tb):
            x_t = as_tile(xw_ref, t0 + t).astype(BF16)
            sums = []
            for g0 in range(0, PEER_ROWS, SUBLANES):
                for j in range(g0 // 2, (g0 + SUBLANES) // 2):
                    prefetch(t, j)
                parts = []
                for j in range(g0, g0 + SUBLANES):
                    p = (buf[t * PEER_ROWS + j, 0:ds, :] * x_t).astype(F32)
                    parts.append(p[0:SUBLANES] + p[SUBLANES:2 * SUBLANES])
                sums.append(_sublane_sums(parts, rows8))
            rowsum = jnp.sum(jnp.concatenate(sums, axis=0), axis=-1, keepdims=True)
            pre = jnp.where(lane == base + t, rowsum, pre)
        w = g_ref[gslot] * _gelu(pre)
        for t in range(tb):
            wcol = jnp.sum(jnp.where(lane == base + t, w, 0.0), axis=-1, keepdims=True)
            wrep_ref[t] = jnp.broadcast_to(wcol, (PEER_ROWS, LANES))
        n_acc = 4
        for t in range(tb):
            accs = [jnp.zeros((ds, LANES), F32) for _ in range(n_acc)]
            for j in range(PEER_ROWS):
                if j % 2 == 0:
                    prefetch(t, half_rows + j // 2)
                v = buf[t * PEER_ROWS + j, ds:2 * ds, :].astype(F32)
                accs[j % n_acc] = accs[j % n_acc] + v * wrep_ref[t, j:j + 1, :]
            hp = as_tile(h_ref, t0 + t) + ((accs[0] + accs[1]) + (accs[2] + accs[3]))
            ms = jnp.mean(jnp.mean(hp * hp, axis=-1, keepdims=True), axis=0, keepdims=True)
            y_t = hp * lax.rsqrt(ms + EPS) * gain
            for s in range(ds):
                y_ref[t0 + t:t0 + t + 1, s * LANES:(s + 1) * LANES] = y_t[s:s + 1, :]

    process(0, sub * step_tokens + tb)

    @pl.when(early_ids)
    def _():
        ids_copy(0, tail).wait()

    process(1, ((sub + 1) % steps_per_blk) * step_tokens)

    @pl.when(late_ids)
    def _():
        ids_copy(tail, TOKEN_BLOCK).wait()

    @pl.when(i == nsteps - 1)
    def _():
        half_wait(0)


def _experts(q3, keys_bf, tab3, xw, h, gf, tb):
    n, d = h.shape
    ds, lanes = d // LANES, LANES
    assert n % TOKEN_BLOCK == 0
    step_tokens = 2 * tb
    nsteps = n // step_tokens
    steps_per_blk = TOKEN_BLOCK // step_tokens
    nblk = n // TOKEN_BLOCK
    dk = q3.shape[2]
    buf = pltpu.VMEM((tb * PEER_ROWS, 2 * ds, lanes), BF16)
    return pl.pallas_call(
        functools.partial(_expert_kernel, tb),
        grid=(nsteps,),
        in_specs=[
            pl.BlockSpec((PEER_HEADS, TOKEN_BLOCK, dk), lambda i: (0, 0, 0)),
            pl.BlockSpec((PEER_HEADS, TOKEN_BLOCK, dk),
                         lambda i: (0, jnp.minimum(i // steps_per_blk + 1, nblk - 1), 0)),
            _const_spec((PEER_HEADS, 2, N_KEYS, N_KEYS)),
            pl.BlockSpec(memory_space=pl.ANY),
            pl.BlockSpec((step_tokens, d), lambda i: (i, 0)),
            pl.BlockSpec((step_tokens, d), lambda i: (i, 0)),
            _const_spec((1, d)),
        ],
        out_specs=pl.BlockSpec((step_tokens, d), lambda i: (i, 0)),
        out_shape=jax.ShapeDtypeStruct((n, d), F32),
        scratch_shapes=[
            buf, buf,
            pltpu.SemaphoreType.DMA((2,)),
            pltpu.VMEM((tb, PEER_ROWS, LANES), F32),
            pltpu.VMEM((PEER_ROWS, TOKEN_BLOCK), F32),
            pltpu.VMEM((2, PEER_ROWS, TOKEN_BLOCK), F32),
            pltpu.VMEM((TOKEN_BLOCK, PEER_ROWS), I32),
            pltpu.SMEM((TOKEN_BLOCK, PEER_ROWS), I32),
            pltpu.SemaphoreType.DMA(()),
        ],
        compiler_params=_cparams(("arbitrary",)),
        name="peer_experts",
    )(q3, q3, keys_bf, tab3, xw, h, gf)


def _row_tile(n, pref):
    t = pref
    while n % t:
        t //= 2
    return t


def kernel(x, meta_tokens, norm_mix_g, w_in, b_in, sinks, conv_w, conv_b, w_r, b_r, w_i, b_i, lru_lambda, gn_attn_g, gn_lru_g, w_out, norm_ffn_g, peer_wq, peer_sub_keys, peer_u, peer_v, final_norm_g):
    n_batch, seq, d = x.shape
    n = n_batch * seq
    assert w_in.shape[0] == 1 and seq % ATT_BLOCK == 0
    x2 = x.reshape(n, d)
    row = lambda a: a.reshape(1, -1)

    w_in_bf = w_in[0].astype(BF16)
    g_mix = row(norm_mix_g[0])
    b_in_r = row(b_in[0])
    qkv, xb, gate = _input_proj(x2, g_mix, w_in_bf, b_in_r, _row_tile(n, 256))
    qkv_m, xb_m, gate_m = _input_proj(meta_tokens.astype(F32), g_mix, w_in_bf, b_in_r, N_META)

    km = qkv_m[:, Q_DIM:Q_DIM + KV_DIM]
    vm = qkv_m[:, Q_DIM + KV_DIM:]
    attn_n = _attention(qkv, km, vm, sinks[0].astype(F32), row(gn_attn_g[0]), n_batch)

    lru_args = (conv_w[0], row(conv_b[0]), _block_diag_groups(w_r[0]).astype(BF16), row(b_r[0]),
                _block_diag_groups(w_i[0]).astype(BF16), row(b_i[0]), row(lru_lambda[0]),
                row(gn_lru_g[0]))
    zeros8 = jnp.zeros((8, LRU_WIDTH), F32)
    _, h_meta = _lru(xb_m, gate_m, *lru_args, zeros8, zeros8, 1, N_META)
    lru_n, _ = _lru(xb, gate, *lru_args, h_meta, xb_m[N_META - 8:], n_batch, _row_tile(seq, 256))

    w_out_bf = w_out[0].astype(BF16)
    h1, xw, q = _mix(attn_n, lru_n, x2, w_out_bf[:Q_DIM], w_out_bf[Q_DIM:], row(norm_ffn_g[0]),
                     peer_wq[0].astype(BF16), _row_tile(n, 256))

    tab3 = _expert_table(peer_u[0], peer_v[0])
    y = _experts(q, peer_sub_keys[0].astype(BF16), tab3, xw, h1, row(final_norm_g), 8)
    return y.reshape(n_batch, seq, d)
```

```python
import functools

import numpy as np
import jax
import jax.numpy as jnp
from jax import lax
from jax.experimental import pallas as pl
from jax.experimental.pallas import tpu as pltpu

F32 = jnp.float32
BF16 = jnp.bfloat16
I32 = jnp.int32

EPS = 1e-6
N_META = 16
HEAD_DIM = 64
N_Q_HEADS = 16
N_KV_HEADS = 2
GROUP = N_Q_HEADS // N_KV_HEADS
Q_DIM = N_Q_HEADS * HEAD_DIM
KV_DIM = N_KV_HEADS * HEAD_DIM
ATT_BLOCK = 128
LRU_WIDTH = 1024
LRU_BLOCKS = 16
LRU_BW = LRU_WIDTH // LRU_BLOCKS
LRU_C = 8.0
GATE_GROUP = 256
PEER_HEADS = 8
N_KEYS = 128
PEER_TOPK = 16
PEER_ROWS = PEER_HEADS * PEER_TOPK
NEG_BIG = -1e30

VMEM_LIMIT = 56 * 1024 * 1024


def _cparams(sem, vmem=VMEM_LIMIT):
    return pltpu.CompilerParams(dimension_semantics=sem, vmem_limit_bytes=vmem)


def _const_spec(shape):
    nd = len(shape)
    return pl.BlockSpec(shape, lambda *_: (0,) * nd, pipeline_mode=pl.Buffered(1))


def _rms(x, g):
    return x * lax.rsqrt(jnp.mean(x * x, axis=-1, keepdims=True) + EPS) * g


def _gelu(x):
    return jax.nn.gelu(x, approximate=True)


def _proj_kernel(x_ref, g_ref, w_ref, b_ref, qkv_ref, xb_ref, gate_ref):
    hn = _rms(x_ref[...], g_ref[...]).astype(BF16)
    n_qkv = Q_DIM + 2 * KV_DIM

    def proj(lo, hi):
        return jnp.dot(hn, w_ref[:, lo:hi], preferred_element_type=F32) + b_ref[:, lo:hi]

    step = 256
    for lo in range(0, n_qkv, step):
        qkv_ref[:, lo:lo + step] = proj(lo, lo + step).astype(BF16)
    for lo in range(0, LRU_WIDTH, step):
        xb_ref[:, lo:lo + step] = proj(n_qkv + lo, n_qkv + lo + step)
        gate_ref[:, lo:lo + step] = proj(n_qkv + LRU_WIDTH + lo, n_qkv + LRU_WIDTH + lo + step)


def _input_proj(x2, g, w_bf, b, tm):
    n, d = x2.shape
    n_qkv = Q_DIM + 2 * KV_DIM
    in_dim = w_bf.shape[1]
    return pl.pallas_call(
        _proj_kernel,
        grid=(n // tm,),
        in_specs=[
            pl.BlockSpec((tm, d), lambda i: (i, 0)),
            _const_spec((1, d)),
            _const_spec((d, in_dim)),
            _const_spec((1, in_dim)),
        ],
        out_specs=[
            pl.BlockSpec((tm, n_qkv), lambda i: (i, 0)),
            pl.BlockSpec((tm, LRU_WIDTH), lambda i: (i, 0)),
            pl.BlockSpec((tm, LRU_WIDTH), lambda i: (i, 0)),
        ],
        out_shape=[
            jax.ShapeDtypeStruct((n, n_qkv), BF16),
            jax.ShapeDtypeStruct((n, LRU_WIDTH), F32),
            jax.ShapeDtypeStruct((n, LRU_WIDTH), F32),
        ],
        compiler_params=_cparams(("parallel",)),
        name="input_proj",
    )(x2, g, w_bf, b)


def _attn_kernel(nb, q_ref, kc_ref, kp_ref, vc_ref, vp_ref, km_ref, vm_ref, sink_ref, g_ref,
                 o_ref, acc_ref):
    blk = pl.program_id(0) % nb
    row = lax.broadcasted_iota(I32, (ATT_BLOCK, 2 * ATT_BLOCK), 0)
    col = lax.broadcasted_iota(I32, (ATT_BLOCK, 2 * ATT_BLOCK), 1)
    no_prev = jnp.where(blk > 0, 0, 2 * ATT_BLOCK)
    in_prev = jnp.logical_and(col < ATT_BLOCK, col > row + no_prev)
    in_cur = jnp.logical_and(col >= ATT_BLOCK, (col - ATT_BLOCK) <= row)
    band = jnp.logical_or(in_prev, in_cur)
    scale = HEAD_DIM ** -0.5
    nt = (((1,), (1,)), ((), ()))
    kvs = []
    for g in range(N_KV_HEADS):
        ks = slice(g * HEAD_DIM, (g + 1) * HEAD_DIM)
        kvs.append((jnp.concatenate([kp_ref[:, ks], kc_ref[:, ks]], axis=0),
                    jnp.concatenate([vp_ref[:, ks], vc_ref[:, ks]], axis=0),
                    km_ref[:, ks], vm_ref[:, ks]))

    def scores(h):
        kb, _, kmg, _ = kvs[h // GROUP]
        qh = q_ref[:, h * HEAD_DIM:(h + 1) * HEAD_DIM]
        sb = lax.dot_general(qh, kb, nt, preferred_element_type=F32) * scale
        sm = lax.dot_general(qh, kmg, nt, preferred_element_type=F32) * scale
        return jnp.where(band, sb, NEG_BIG), sm

    nxt = scores(0)
    for h in range(N_Q_HEADS):
        sb, sm = nxt
        if h + 1 < N_Q_HEADS:
            nxt = scores(h + 1)
        _, vb, _, vmg = kvs[h // GROUP]
        sink = sink_ref[h]
        m = jnp.maximum(jnp.max(sb, axis=-1, keepdims=True), jnp.max(sm, axis=-1, keepdims=True))
        m = jnp.maximum(m, sink)
        pb = jnp.exp(sb - m)
        pm = jnp.exp(sm - m)
        den = (jnp.sum(pb, axis=-1, keepdims=True) + jnp.sum(pm, axis=-1, keepdims=True)
               + jnp.exp(sink - m))
        o = (jnp.dot(pb.astype(BF16), vb, preferred_element_type=F32)
             + jnp.dot(pm.astype(BF16), vmg, preferred_element_type=F32))
        acc_ref[:, h * HEAD_DIM:(h + 1) * HEAD_DIM] = o / den
    o_ref[...] = _rms(acc_ref[...], g_ref[...]).astype(BF16)


def _attention(qkv, km, vm, sinks, g, n_batch):
    n = qkv.shape[0]
    nblk = n // ATT_BLOCK
    nb = nblk // n_batch
    kcol = Q_DIM // KV_DIM
    return pl.pallas_call(
        functools.partial(_attn_kernel, nb),
        grid=(nblk,),
        in_specs=[
            pl.BlockSpec((ATT_BLOCK, Q_DIM), lambda i: (i, 0)),
            pl.BlockSpec((ATT_BLOCK, KV_DIM), lambda i: (i, kcol)),
            pl.BlockSpec((ATT_BLOCK, KV_DIM), lambda i: (jnp.maximum(i - 1, 0), kcol)),
            pl.BlockSpec((ATT_BLOCK, KV_DIM), lambda i: (i, kcol + 1)),
            pl.BlockSpec((ATT_BLOCK, KV_DIM), lambda i: (jnp.maximum(i - 1, 0), kcol + 1)),
            _const_spec((N_META, KV_DIM)),
            _const_spec((N_META, KV_DIM)),
            pl.BlockSpec(memory_space=pltpu.SMEM),
            _const_spec((1, Q_DIM)),
        ],
        out_specs=pl.BlockSpec((ATT_BLOCK, Q_DIM), lambda i: (i, 0)),
        out_shape=jax.ShapeDtypeStruct((n, Q_DIM), BF16),
        scratch_shapes=[pltpu.VMEM((ATT_BLOCK, Q_DIM), F32)],
        compiler_params=_cparams(("parallel",)),
        name="swa_attention",
    )(qkv, qkv, qkv, qkv, qkv, km, vm, sinks, g)


def _log_sigmoid(x):
    return -(jnp.maximum(-x, 0.0) + jnp.log1p(jnp.exp(-jnp.abs(x))))


def _lru_kernel(tt, x_ref, gate_ref, cw_ref, cb_ref, wr_ref, br_ref, wi_ref, bi_ref, lam_ref, g_ref,
                h0_ref, tail0_ref, o_ref, hlast_ref, xfull_ref, h_ref):
    t = pl.program_id(1)

    @pl.when(t == 0)
    def _():
        h_ref[...] = h0_ref[...]
        xfull_ref[0:8, :] = tail0_ref[...]

    x = x_ref[...]
    xfull_ref[8:8 + tt, :] = x
    xc = (cb_ref[...]
          + cw_ref[0:1, :] * xfull_ref[5:5 + tt, :]
          + cw_ref[1:2, :] * xfull_ref[6:6 + tt, :]
          + cw_ref[2:3, :] * xfull_ref[7:7 + tt, :]
          + cw_ref[3:4, :] * x)
    xfull_ref[0:8, :] = x[tt - 8:tt, :]

    xcb = xc.astype(BF16)
    r_parts, i_parts = [], []
    for gi in range(LRU_WIDTH // GATE_GROUP):
        sl = slice(gi * GATE_GROUP, (gi + 1) * GATE_GROUP)
        r_parts.append(jnp.dot(xcb[:, sl], wr_ref[gi], preferred_element_type=F32))
        i_parts.append(jnp.dot(xcb[:, sl], wi_ref[gi], preferred_element_type=F32))
    r = jax.nn.sigmoid(jnp.concatenate(r_parts, axis=-1) + br_ref[...])
    ig = jax.nn.sigmoid(jnp.concatenate(i_parts, axis=-1) + bi_ref[...])
    log_a = LRU_C * r * _log_sigmoid(lam_ref[...])
    a = jnp.exp(log_a)
    th = jnp.tanh(log_a)
    b = jnp.sqrt(-2.0 * th / (1.0 - th)) * (ig * xc)

    rows = lax.broadcasted_iota(I32, a.shape, 0)
    s = 1
    while s < tt:
        live = rows >= s
        a_sh = pltpu.roll(a, s, 0)
        b_sh = pltpu.roll(b, s, 0)
        b = jnp.where(live, a * b_sh + b, b)
        a = jnp.where(live, a * a_sh, a)
        s *= 2
    h = b + a * h_ref[0:1, :]
    hl = h[tt - 1:tt, :]
    h_ref[...] = jnp.broadcast_to(hl, h_ref.shape)
    hlast_ref[...] = jnp.broadcast_to(hl, hlast_ref.shape)

    o_ref[...] = _rms(_gelu(gate_ref[...]) * h, g_ref[...]).astype(BF16)


def _lru(xb, gate, cw, cb, wr_bd, br, wi_bd, bi, lam, g, h0, tail0, n_batch, tt):
    n = xb.shape[0]
    nt = n // n_batch // tt
    w = LRU_WIDTH
    ng = w // GATE_GROUP
    row_spec = pl.BlockSpec((tt, w), lambda b, t: (b * nt + t, 0))
    return pl.pallas_call(
        functools.partial(_lru_kernel, tt),
        grid=(n_batch, nt),
        in_specs=[
            row_spec, row_spec,
            _const_spec((4, w)), _const_spec((1, w)),
            _const_spec((ng, GATE_GROUP, GATE_GROUP)), _const_spec((1, w)),
            _const_spec((ng, GATE_GROUP, GATE_GROUP)), _const_spec((1, w)),
            _const_spec((1, w)), _const_spec((1, w)),
            _const_spec((8, w)), _const_spec((8, w)),
        ],
        out_specs=[row_spec, pl.BlockSpec((8, w), lambda b, t: (b * nt + t, 0))],
        out_shape=[jax.ShapeDtypeStruct((n, w), BF16),
                   jax.ShapeDtypeStruct((n_batch * nt * 8, w), F32)],
        scratch_shapes=[pltpu.VMEM((tt + 8, w), F32), pltpu.VMEM((8, w), F32)],
        compiler_params=_cparams(("arbitrary", "arbitrary")),
        name="rg_lru",
    )(xb, gate, cw, cb, wr_bd, br, wi_bd, bi, lam, g, h0, tail0)


def _block_diag_groups(w):
    per = GATE_GROUP // LRU_BW
    w4 = w.reshape(LRU_BLOCKS // per, per, LRU_BW, LRU_BW)
    eye = jnp.eye(per, dtype=w.dtype)
    return jnp.einsum('gacd,ab->gacbd', w4, eye).reshape(LRU_BLOCKS // per, GATE_GROUP, GATE_GROUP)


def _top16_rows(s, ids):
    vals, picks = [], []
    for _ in range(PEER_TOPK):
        m = jnp.max(s, axis=0, keepdims=True)
        pick = jnp.min(jnp.where(s == m, ids, jnp.inf), axis=0, keepdims=True)
        s = jnp.where(ids == pick, -jnp.inf, s)
        vals.append(m)
        picks.append(pick)
    return jnp.concatenate(vals, axis=0), jnp.concatenate(picks, axis=0)


def _peer_select(qh, keys):
    tm = qh.shape[0]
    nt = (((1,), (1,)), ((), ()))
    key_ids = lax.broadcasted_iota(I32, (N_KEYS, tm), 0).astype(F32)
    tops = []
    for p in range(2):
        qp = qh[:, p * N_KEYS:(p + 1) * N_KEYS]
        s = lax.dot_general(keys[p], qp, nt, preferred_element_type=F32)
        tops.append(_top16_rows(s, key_ids))
    (v0, i0), (v1, i1) = tops
    sub8 = lax.broadcasted_iota(I32, (8, tm), 0).astype(F32)
    cand = [v0[0:1] + v1]
    cid = [lax.broadcasted_iota(I32, (PEER_TOPK, tm), 0).astype(F32)]
    eid = [i0[0:1] * N_KEYS + i1]
    for k1 in range(1, PEER_TOPK):
        cand.append(v0[k1:k1 + 1] + v1[0:8])
        cid.append(sub8 + float(k1 * PEER_TOPK))
        eid.append(i0[k1:k1 + 1] * N_KEYS + i1[0:8])
    cand = jnp.concatenate(cand, axis=0)
    cid = jnp.concatenate(cid, axis=0)
    eid = jnp.concatenate(eid, axis=0)
    best, experts = [], []
    for _ in range(PEER_TOPK):
        m = jnp.max(cand, axis=0, keepdims=True)
        pick = jnp.min(jnp.where(cand == m, cid, jnp.inf), axis=0, keepdims=True)
        sel = cid == pick
        experts.append(jnp.max(jnp.where(sel, eid, -1.0), axis=0, keepdims=True))
        cand = jnp.where(sel, -jnp.inf, cand)
        best.append(m)
    best = jnp.concatenate(best, axis=0)
    ex = jnp.exp(best - best[0:1])
    return jnp.concatenate(experts, axis=0), ex / jnp.sum(ex, axis=0, keepdims=True)


def _expert_table(u, v):
    ne, d = u.shape
    return jnp.concatenate([u.astype(BF16), v.astype(BF16)], axis=1).reshape(ne, 2 * d // 128, 128)


SUBLANES = 8
LANES = 128


def _fold_pair(x, y, sh, rows):
    x2 = x + pltpu.roll(x, sh, 0)
    y2 = y + pltpu.roll(y, SUBLANES - sh, 0)
    return jnp.where((rows // sh) % 2 == 1, x2, y2)


def _fold_order():
    rows = np.arange(SUBLANES)[:, None]

    def fold(x, y, sh):
        x2 = x + np.roll(x, sh, 0)
        y2 = y + np.roll(y, SUBLANES - sh, 0)
        return np.where((rows // sh) % 2 == 1, x2, y2)

    vals = []
    for i in range(SUBLANES):
        v = np.zeros((SUBLANES, SUBLANES * SUBLANES), np.int64)
        v[np.arange(SUBLANES), i * SUBLANES + np.arange(SUBLANES)] = 1
        vals.append(v)
    sh = SUBLANES // 2
    while len(vals) > 1:
        vals = [fold(vals[2 * k], vals[2 * k + 1], sh) for k in range(len(vals) // 2)]
        sh //= 2
    out = vals[0].reshape(SUBLANES, SUBLANES, SUBLANES)
    src_of_row = [int(np.argmax(out[r].sum(-1))) for r in range(SUBLANES)]
    for r in range(SUBLANES):
        assert (out[r, src_of_row[r]] == 1).all() and out[r].sum() == SUBLANES
    order = [0] * SUBLANES
    for r, p in enumerate(src_of_row):
        order[p] = r
    return tuple(order)


_FOLD_ORDER = _fold_order()


def _sublane_sums(parts, rows):
    vals = [parts[r] for r in _FOLD_ORDER]
    sh = SUBLANES // 2
    while len(vals) > 1:
        vals = [_fold_pair(vals[2 * k], vals[2 * k + 1], sh, rows) for k in range(len(vals) // 2)]
        sh //= 2
    return vals[0]


TOKEN_BLOCK = 128


def _expert_kernel(tb, a0_ref, l0_ref, x0_ref, an_ref, ln_ref, xn_ref, wa_ref, wl_ref, gn_ref, wq_ref,
                   keys_ref, tab_ref, gf_ref,
                   y_ref, buf_a_ref, buf_b_ref, sem_ref, wrep_ref, e_ref, g_ref, idxv_ref, idx_ref,
                   isem_ref, h_ref, xw_ref, xb_ref):
    i = pl.program_id(0)
    nsteps = pl.num_programs(0)
    step_tokens = 2 * tb
    steps_per_blk = TOKEN_BLOCK // step_tokens
    assert steps_per_blk == PEER_HEADS
    blk = i // steps_per_blk
    sub = i % steps_per_blk
    nblk = nsteps // steps_per_blk
    rows_per_half = tb * PEER_ROWS
    ds = xw_ref.shape[2] // LANES
    bufs = (buf_a_ref, buf_b_ref)

    def as_tile(rows, r):
        return jnp.concatenate([rows[r:r + 1, s * LANES:(s + 1) * LANES] for s in range(ds)], axis=0)

    tail = TOKEN_BLOCK - step_tokens

    def row_copy(e, half, r):
        return pltpu.make_async_copy(tab_ref.at[e], bufs[half].at[r], sem_ref.at[half])

    def half_wait(half):
        pltpu.make_async_copy(tab_ref.at[pl.ds(0, rows_per_half)], bufs[half], sem_ref.at[half]).wait()

    def out_proj(a_ref, l_ref, x_ref, slot):
        h = (x_ref[...]
             + jnp.dot(a_ref[...], wa_ref[...], preferred_element_type=F32)
             + jnp.dot(l_ref[...], wl_ref[...], preferred_element_type=F32))
        hn = _rms(h, gn_ref[...])
        h_ref[slot] = h
        xw_ref[slot] = hn
        xb_ref[slot] = hn.astype(BF16)

    def select_head(hd, slot):
        q = jnp.dot(xb_ref[slot], wq_ref[hd], preferred_element_type=F32).astype(BF16)
        experts, gates = _peer_select(q, keys_ref[hd])
        r0 = pl.multiple_of(hd * PEER_TOPK, PEER_TOPK)
        e_ref[pl.ds(r0, PEER_TOPK), :] = experts
        g_ref[slot, pl.ds(r0, PEER_TOPK), :] = gates

    def ids_copy(lo, hi):
        return pltpu.make_async_copy(idxv_ref.at[pl.ds(lo, hi - lo)], idx_ref.at[pl.ds(lo, hi - lo)], isem_ref)

    @pl.when(i == 0)
    def _():
        out_proj(a0_ref, l0_ref, x0_ref, 0)

        def head_body(hd, carry):
            select_head(hd, 0)
            return carry
        lax.fori_loop(0, PEER_HEADS, head_body, 0)
        idxv_ref[...] = jnp.transpose(e_ref[...]).astype(I32)
        ids_copy(0, TOKEN_BLOCK).start()
        ids_copy(0, TOKEN_BLOCK).wait()

        def body(t, carry):
            for j in range(PEER_ROWS):
                row_copy(idx_ref[t, j], 0, t * PEER_ROWS + j).start()
            return carry
        lax.fori_loop(0, tb, body, 0)

    late_ids = jnp.logical_and(sub == 0, blk > 0)
    early_ids = jnp.logical_and(sub == steps_per_blk - 1, blk + 1 < nblk)

    @pl.when(late_ids)
    def _():
        ids_copy(tail, TOKEN_BLOCK).start()

    nslot = (blk + 1) % 2

    @pl.when(jnp.logical_and(sub == 0, blk + 1 < nblk))
    def _():
        out_proj(an_ref, ln_ref, xn_ref, nslot)

    @pl.when(blk + 1 < nblk)
    def _():
        select_head(sub, nslot)

    @pl.when(early_ids)
    def _():
        idxv_ref[...] = jnp.transpose(e_ref[...]).astype(I32)
        ids_copy(0, tail).start()

    rows8 = lax.broadcasted_iota(I32, (SUBLANES, LANES), 0)
    lane = lax.broadcasted_iota(I32, (PEER_ROWS, TOKEN_BLOCK), 1)
    half_rows = PEER_ROWS // 2
    gslot = blk % 2
    gain = as_tile(gf_ref[...], 0)

    def process(half, next_row0):
        other = 1 - half
        t0 = half * tb
        half_wait(half)
        buf = bufs[half]
        base = sub * step_tokens + t0

        def prefetch(t, j):
            row_copy(idx_ref[next_row0 + t, j], other, t * PEER_ROWS + j).start(priority=j % 2)

        assert tb == SUBLANES
        row0 = pl.multiple_of(base, SUBLANES)
        x_rows = xw_ref[gslot, pl.ds(row0, tb), :]
        h_rows = h_ref[gslot, pl.ds(row0, tb), :]
        pre = jnp.zeros((PEER_ROWS, TOKEN_BLOCK), F32)
        for t in range(tb):
            x_t = as_tile(x_rows, t).astype(BF16)
            sums = []
            for g0 in range(0, PEER_ROWS, SUBLANES):
                for j in range(g0 // 2, (g0 + SUBLANES) // 2):
                    prefetch(t, j)
                parts = []
                for j in range(g0, g0 + SUBLANES):
                    p = (buf[t * PEER_ROWS + j, 0:ds, :] * x_t).astype(F32)
                    parts.append(p[0:SUBLANES] + p[SUBLANES:2 * SUBLANES])
                sums.append(_sublane_sums(parts, rows8))
            rowsum = jnp.sum(jnp.concatenate(sums, axis=0), axis=-1, keepdims=True)
            pre = jnp.where(lane == base + t, rowsum, pre)
        w = g_ref[gslot] * _gelu(pre)
        for t in range(tb):
            wcol = jnp.sum(jnp.where(lane == base + t, w, 0.0), axis=-1, keepdims=True)
            wrep_ref[t] = jnp.broadcast_to(wcol, (PEER_ROWS, LANES))
        n_acc = 4
        for t in range(tb):
            accs = [jnp.zeros((ds, LANES), F32) for _ in range(n_acc)]
            for j in range(PEER_ROWS):
                if j % 2 == 0:
                    prefetch(t, half_rows + j // 2)
                v = buf[t * PEER_ROWS + j, ds:2 * ds, :].astype(F32)
                accs[j % n_acc] = accs[j % n_acc] + v * wrep_ref[t, j:j + 1, :]
            hp = as_tile(h_rows, t) + ((accs[0] + accs[1]) + (accs[2] + accs[3]))
            ms = jnp.mean(jnp.mean(hp * hp, axis=-1, keepdims=True), axis=0, keepdims=True)
            y_t = hp * lax.rsqrt(ms + EPS) * gain
            for s in range(ds):
                y_ref[t0 + t:t0 + t + 1, s * LANES:(s + 1) * LANES] = y_t[s:s + 1, :]

    process(0, sub * step_tokens + tb)

    @pl.when(early_ids)
    def _():
        ids_copy(0, tail).wait()

    process(1, ((sub + 1) % steps_per_blk) * step_tokens)

    @pl.when(late_ids)
    def _():
        ids_copy(tail, TOKEN_BLOCK).wait()

    @pl.when(i == nsteps - 1)
    def _():
        half_wait(0)


def _experts(attn_n, lru_n, x2, wa, wl, gn, wq3, keys_bf, tab3, gf, tb):
    n, d = x2.shape
    ds, lanes = d // LANES, LANES
    assert n % TOKEN_BLOCK == 0
    step_tokens = 2 * tb
    nsteps = n // step_tokens
    steps_per_blk = TOKEN_BLOCK // step_tokens
    nblk = n // TOKEN_BLOCK
    dk = wq3.shape[2]
    buf = pltpu.VMEM((tb * PEER_ROWS, 2 * ds, lanes), BF16)

    def first(width):
        return pl.BlockSpec((TOKEN_BLOCK, width), lambda i: (0, 0))

    def ahead(width):
        return pl.BlockSpec((TOKEN_BLOCK, width), lambda i: (jnp.minimum(i // steps_per_blk + 1, nblk - 1), 0))

    return pl.pallas_call(
        functools.partial(_expert_kernel, tb),
        grid=(nsteps,),
        in_specs=[
            first(Q_DIM), first(LRU_WIDTH), first(d),
            ahead(Q_DIM), ahead(LRU_WIDTH), ahead(d),
            _const_spec((Q_DIM, d)), _const_spec((LRU_WIDTH, d)), _const_spec((1, d)),
            _const_spec((PEER_HEADS, d, dk)),
            _const_spec((PEER_HEADS, 2, N_KEYS, N_KEYS)),
            pl.BlockSpec(memory_space=pl.ANY),
            _const_spec((1, d)),
        ],
        out_specs=pl.BlockSpec((step_tokens, d), lambda i: (i, 0)),
        out_shape=jax.ShapeDtypeStruct((n, d), F32),
        scratch_shapes=[
            buf, buf,
            pltpu.SemaphoreType.DMA((2,)),
            pltpu.VMEM((tb, PEER_ROWS, LANES), F32),
            pltpu.VMEM((PEER_ROWS, TOKEN_BLOCK), F32),
            pltpu.VMEM((2, PEER_ROWS, TOKEN_BLOCK), F32),
            pltpu.VMEM((TOKEN_BLOCK, PEER_ROWS), I32),
            pltpu.SMEM((TOKEN_BLOCK, PEER_ROWS), I32),
            pltpu.SemaphoreType.DMA(()),
            pltpu.VMEM((2, TOKEN_BLOCK, d), F32),
            pltpu.VMEM((2, TOKEN_BLOCK, d), F32),
            pltpu.VMEM((2, TOKEN_BLOCK, d), BF16),
        ],
        compiler_params=_cparams(("arbitrary",)),
        name="peer_experts",
    )(attn_n, lru_n, x2, attn_n, lru_n, x2, wa, wl, gn, wq3, keys_bf, tab3, gf)


def _row_tile(n, pref):
    t = pref
    while n % t:
        t //= 2
    return t


def kernel(x, meta_tokens, norm_mix_g, w_in, b_in, sinks, conv_w, conv_b, w_r, b_r, w_i, b_i, lru_lambda, gn_attn_g, gn_lru_g, w_out, norm_ffn_g, peer_wq, peer_sub_keys, peer_u, peer_v, final_norm_g):
    n_batch, seq, d = x.shape
    n = n_batch * seq
    assert w_in.shape[0] == 1 and seq % ATT_BLOCK == 0
    x2 = x.reshape(n, d)
    row = lambda a: a.reshape(1, -1)

    w_in_bf = w_in[0].astype(BF16)
    g_mix = row(norm_mix_g[0])
    b_in_r = row(b_in[0])
    qkv, xb, gate = _input_proj(x2, g_mix, w_in_bf, b_in_r, _row_tile(n, 256))
    qkv_m, xb_m, gate_m = _input_proj(meta_tokens.astype(F32), g_mix, w_in_bf, b_in_r, N_META)

    km = qkv_m[:, Q_DIM:Q_DIM + KV_DIM]
    vm = qkv_m[:, Q_DIM + KV_DIM:]
    attn_n = _attention(qkv, km, vm, sinks[0].astype(F32), row(gn_attn_g[0]), n_batch)

    lru_args = (conv_w[0], row(conv_b[0]), _block_diag_groups(w_r[0]).astype(BF16), row(b_r[0]),
                _block_diag_groups(w_i[0]).astype(BF16), row(b_i[0]), row(lru_lambda[0]),
                row(gn_lru_g[0]))
    zeros8 = jnp.zeros((8, LRU_WIDTH), F32)
    _, h_meta = _lru(xb_m, gate_m, *lru_args, zeros8, zeros8, 1, N_META)
    lru_n, _ = _lru(xb, gate, *lru_args, h_meta, xb_m[N_META - 8:], n_batch, _row_tile(seq, 256))

    w_out_bf = w_out[0].astype(BF16)
    dq = peer_wq.shape[2]
    wq3 = peer_wq[0].astype(BF16).reshape(d, PEER_HEADS, dq // PEER_HEADS).transpose(1, 0, 2)
    tab3 = _expert_table(peer_u[0], peer_v[0])
    y = _experts(attn_n, lru_n, x2, w_out_bf[:Q_DIM], w_out_bf[Q_DIM:], row(norm_ffn_g[0]), wq3,
                 peer_sub_keys[0].astype(BF16), tab3, row(final_norm_g), 8)
    return y.reshape(n_batch, seq, d)
```

```python
import functools

import numpy as np
import jax
import jax.numpy as jnp
from jax import lax
from jax.experimental import pallas as pl
from jax.experimental.pallas import tpu as pltpu

F32 = jnp.float32
BF16 = jnp.bfloat16
I32 = jnp.int32

EPS = 1e-6
N_META = 16
HEAD_DIM = 64
N_Q_HEADS = 16
N_KV_HEADS = 2
GROUP = N_Q_HEADS // N_KV_HEADS
Q_DIM = N_Q_HEADS * HEAD_DIM
KV_DIM = N_KV_HEADS * HEAD_DIM
ATT_BLOCK = 128
LRU_WIDTH = 1024
LRU_BLOCKS = 16
LRU_BW = LRU_WIDTH // LRU_BLOCKS
LRU_C = 8.0
GATE_GROUP = 256
PEER_HEADS = 8
N_KEYS = 128
PEER_TOPK = 16
PEER_ROWS = PEER_HEADS * PEER_TOPK
NEG_BIG = -1e30

VMEM_LIMIT = 56 * 1024 * 1024


def _cparams(sem, vmem=VMEM_LIMIT):
    return pltpu.CompilerParams(dimension_semantics=sem, vmem_limit_bytes=vmem)


def _const_spec(shape):
    nd = len(shape)
    return pl.BlockSpec(shape, lambda *_: (0,) * nd, pipeline_mode=pl.Buffered(1))


def _rms(x, g):
    return x * lax.rsqrt(jnp.mean(x * x, axis=-1, keepdims=True) + EPS) * g


def _gelu(x):
    return jax.nn.gelu(x, approximate=True)


def _proj_kernel(x_ref, g_ref, w_ref, b_ref, qkv_ref, xb_ref, gate_ref):
    hn = _rms(x_ref[...], g_ref[...]).astype(BF16)
    n_qkv = Q_DIM + 2 * KV_DIM

    def proj(lo, hi):
        return jnp.dot(hn, w_ref[:, lo:hi], preferred_element_type=F32) + b_ref[:, lo:hi]

    step = 256
    for lo in range(0, n_qkv, step):
        qkv_ref[:, lo:lo + step] = proj(lo, lo + step).astype(BF16)
    for lo in range(0, LRU_WIDTH, step):
        xb_ref[:, lo:lo + step] = proj(n_qkv + lo, n_qkv + lo + step)
        gate_ref[:, lo:lo + step] = proj(n_qkv + LRU_WIDTH + lo, n_qkv + LRU_WIDTH + lo + step)


def _input_proj(x2, g, w_bf, b, tm):
    n, d = x2.shape
    n_qkv = Q_DIM + 2 * KV_DIM
    in_dim = w_bf.shape[1]
    return pl.pallas_call(
        _proj_kernel,
        grid=(n // tm,),
        in_specs=[
            pl.BlockSpec((tm, d), lambda i: (i, 0)),
            _const_spec((1, d)),
            _const_spec((d, in_dim)),
            _const_spec((1, in_dim)),
        ],
        out_specs=[
            pl.BlockSpec((tm, n_qkv), lambda i: (i, 0)),
            pl.BlockSpec((tm, LRU_WIDTH), lambda i: (i, 0)),
            pl.BlockSpec((tm, LRU_WIDTH), lambda i: (i, 0)),
        ],
        out_shape=[
            jax.ShapeDtypeStruct((n, n_qkv), BF16),
            jax.ShapeDtypeStruct((n, LRU_WIDTH), F32),
            jax.ShapeDtypeStruct((n, LRU_WIDTH), F32),
        ],
        compiler_params=_cparams(("parallel",)),
        name="input_proj",
    )(x2, g, w_bf, b)


def _attn_kernel(nb, q_ref, kc_ref, kp_ref, vc_ref, vp_ref, km_ref, vm_ref, sink_ref, g_ref,
                 o_ref, acc_ref):
    blk = pl.program_id(0) % nb
    row = lax.broadcasted_iota(I32, (ATT_BLOCK, 2 * ATT_BLOCK), 0)
    col = lax.broadcasted_iota(I32, (ATT_BLOCK, 2 * ATT_BLOCK), 1)
    no_prev = jnp.where(blk > 0, 0, 2 * ATT_BLOCK)
    in_prev = jnp.logical_and(col < ATT_BLOCK, col > row + no_prev)
    in_cur = jnp.logical_and(col >= ATT_BLOCK, (col - ATT_BLOCK) <= row)
    band = jnp.logical_or(in_prev, in_cur)
    scale = HEAD_DIM ** -0.5
    nt = (((1,), (1,)), ((), ()))
    kvs = []
    for g in range(N_KV_HEADS):
        ks = slice(g * HEAD_DIM, (g + 1) * HEAD_DIM)
        kvs.append((jnp.concatenate([kp_ref[:, ks], kc_ref[:, ks]], axis=0),
                    jnp.concatenate([vp_ref[:, ks], vc_ref[:, ks]], axis=0),
                    km_ref[:, ks], vm_ref[:, ks]))

    def scores(h):
        kb, _, kmg, _ = kvs[h // GROUP]
        qh = q_ref[:, h * HEAD_DIM:(h + 1) * HEAD_DIM]
        sb = lax.dot_general(qh, kb, nt, preferred_element_type=F32) * scale
        sm = lax.dot_general(qh, kmg, nt, preferred_element_type=F32) * scale
        return jnp.where(band, sb, NEG_BIG), sm

    nxt = scores(0)
    for h in range(N_Q_HEADS):
        sb, sm = nxt
        if h + 1 < N_Q_HEADS:
            nxt = scores(h + 1)
        _, vb, _, vmg = kvs[h // GROUP]
        sink = sink_ref[h]
        m = jnp.maximum(jnp.max(sb, axis=-1, keepdims=True), jnp.max(sm, axis=-1, keepdims=True))
        m = jnp.maximum(m, sink)
        pb = jnp.exp(sb - m)
        pm = jnp.exp(sm - m)
        den = (jnp.sum(pb, axis=-1, keepdims=True) + jnp.sum(pm, axis=-1, keepdims=True)
               + jnp.exp(sink - m))
        o = (jnp.dot(pb.astype(BF16), vb, preferred_element_type=F32)
             + jnp.dot(pm.astype(BF16), vmg, preferred_element_type=F32))
        acc_ref[:, h * HEAD_DIM:(h + 1) * HEAD_DIM] = o / den
    o_ref[...] = _rms(acc_ref[...], g_ref[...]).astype(BF16)


def _attention(qkv, km, vm, sinks, g, n_batch):
    n = qkv.shape[0]
    nblk = n // ATT_BLOCK
    nb = nblk // n_batch
    kcol = Q_DIM // KV_DIM
    return pl.pallas_call(
        functools.partial(_attn_kernel, nb),
        grid=(nblk,),
        in_specs=[
            pl.BlockSpec((ATT_BLOCK, Q_DIM), lambda i: (i, 0)),
            pl.BlockSpec((ATT_BLOCK, KV_DIM), lambda i: (i, kcol)),
            pl.BlockSpec((ATT_BLOCK, KV_DIM), lambda i: (jnp.maximum(i - 1, 0), kcol)),
            pl.BlockSpec((ATT_BLOCK, KV_DIM), lambda i: (i, kcol + 1)),
            pl.BlockSpec((ATT_BLOCK, KV_DIM), lambda i: (jnp.maximum(i - 1, 0), kcol + 1)),
            _const_spec((N_META, KV_DIM)),
            _const_spec((N_META, KV_DIM)),
            pl.BlockSpec(memory_space=pltpu.SMEM),
            _const_spec((1, Q_DIM)),
        ],
        out_specs=pl.BlockSpec((ATT_BLOCK, Q_DIM), lambda i: (i, 0)),
        out_shape=jax.ShapeDtypeStruct((n, Q_DIM), BF16),
        scratch_shapes=[pltpu.VMEM((ATT_BLOCK, Q_DIM), F32)],
        compiler_params=_cparams(("parallel",)),
        name="swa_attention",
    )(qkv, qkv, qkv, qkv, qkv, km, vm, sinks, g)


def _log_sigmoid(x):
    return -(jnp.maximum(-x, 0.0) + jnp.log1p(jnp.exp(-jnp.abs(x))))


def _lru_kernel(tt, x_ref, gate_ref, cw_ref, cb_ref, wr_ref, br_ref, wi_ref, bi_ref, lam_ref, g_ref,
                h0_ref, tail0_ref, o_ref, hlast_ref, xfull_ref, h_ref):
    t = pl.program_id(1)

    @pl.when(t == 0)
    def _():
        h_ref[...] = h0_ref[...]
        xfull_ref[0:8, :] = tail0_ref[...]

    x = x_ref[...]
    xfull_ref[8:8 + tt, :] = x
    xc = (cb_ref[...]
          + cw_ref[0:1, :] * xfull_ref[5:5 + tt, :]
          + cw_ref[1:2, :] * xfull_ref[6:6 + tt, :]
          + cw_ref[2:3, :] * xfull_ref[7:7 + tt, :]
          + cw_ref[3:4, :] * x)
    xfull_ref[0:8, :] = x[tt - 8:tt, :]

    xcb = xc.astype(BF16)
    r_parts, i_parts = [], []
    for gi in range(LRU_WIDTH // GATE_GROUP):
        sl = slice(gi * GATE_GROUP, (gi + 1) * GATE_GROUP)
        r_parts.append(jnp.dot(xcb[:, sl], wr_ref[gi], preferred_element_type=F32))
        i_parts.append(jnp.dot(xcb[:, sl], wi_ref[gi], preferred_element_type=F32))
    r = jax.nn.sigmoid(jnp.concatenate(r_parts, axis=-1) + br_ref[...])
    ig = jax.nn.sigmoid(jnp.concatenate(i_parts, axis=-1) + bi_ref[...])
    log_a = LRU_C * r * _log_sigmoid(lam_ref[...])
    a = jnp.exp(log_a)
    th = jnp.tanh(log_a)
    b = jnp.sqrt(-2.0 * th / (1.0 - th)) * (ig * xc)

    grp = 8
    sub = lax.broadcasted_iota(I32, a.shape, 0) % grp
    s = 1
    while s < grp:
        live = sub >= s
        a_sh = pltpu.roll(a, s, 0)
        b_sh = pltpu.roll(b, s, 0)
        b = jnp.where(live, a * b_sh + b, b)
        a = jnp.where(live, a * a_sh, a)
        s *= 2
    hl = h_ref[0:1, :]
    groups = []
    for k in range(tt // grp):
        hk = b[k * grp:(k + 1) * grp] + a[k * grp:(k + 1) * grp] * hl
        groups.append(hk)
        hl = hk[grp - 1:grp]
    h = jnp.concatenate(groups, axis=0)
    h_ref[...] = jnp.broadcast_to(hl, h_ref.shape)
    hlast_ref[...] = jnp.broadcast_to(hl, hlast_ref.shape)

    o_ref[...] = _rms(_gelu(gate_ref[...]) * h, g_ref[...]).astype(BF16)


def _lru(xb, gate, cw, cb, wr_bd, br, wi_bd, bi, lam, g, h0, tail0, n_batch, tt):
    n = xb.shape[0]
    nt = n // n_batch // tt
    w = LRU_WIDTH
    ng = w // GATE_GROUP
    row_spec = pl.BlockSpec((tt, w), lambda b, t: (b * nt + t, 0))
    return pl.pallas_call(
        functools.partial(_lru_kernel, tt),
        grid=(n_batch, nt),
        in_specs=[
            row_spec, row_spec,
            _const_spec((4, w)), _const_spec((1, w)),
            _const_spec((ng, GATE_GROUP, GATE_GROUP)), _const_spec((1, w)),
            _const_spec((ng, GATE_GROUP, GATE_GROUP)), _const_spec((1, w)),
            _const_spec((1, w)), _const_spec((1, w)),
            _const_spec((8, w)), _const_spec((8, w)),
        ],
        out_specs=[row_spec, pl.BlockSpec((8, w), lambda b, t: (b * nt + t, 0))],
        out_shape=[jax.ShapeDtypeStruct((n, w), BF16),
                   jax.ShapeDtypeStruct((n_batch * nt * 8, w), F32)],
        scratch_shapes=[pltpu.VMEM((tt + 8, w), F32), pltpu.VMEM((8, w), F32)],
        compiler_params=_cparams(("arbitrary", "arbitrary")),
        name="rg_lru",
    )(xb, gate, cw, cb, wr_bd, br, wi_bd, bi, lam, g, h0, tail0)


def _block_diag_groups(w):
    per = GATE_GROUP // LRU_BW
    w4 = w.reshape(LRU_BLOCKS // per, per, LRU_BW, LRU_BW)
    eye = jnp.eye(per, dtype=w.dtype)
    return jnp.einsum('gacd,ab->gacbd', w4, eye).reshape(LRU_BLOCKS // per, GATE_GROUP, GATE_GROUP)


def _mix_kernel(a_ref, l_ref, x_ref, wa_ref, wl_ref, g_ref, wq_ref, h_ref, xw_ref, q_ref):
    h = (x_ref[...]
         + jnp.dot(a_ref[...], wa_ref[...], preferred_element_type=F32)
         + jnp.dot(l_ref[...], wl_ref[...], preferred_element_type=F32))
    h_ref[...] = h
    hn = _rms(h, g_ref[...])
    xw_ref[...] = hn
    q = jnp.dot(hn.astype(BF16), wq_ref[...], preferred_element_type=F32).astype(BF16)
    dk = q_ref.shape[2]
    for hd in range(q_ref.shape[0]):
        q_ref[hd] = q[:, hd * dk:(hd + 1) * dk]


def _mix(attn_n, lru_n, x2, wa, wl, g, wq, tm):
    n, d = x2.shape
    dq = wq.shape[1]
    return pl.pallas_call(
        _mix_kernel,
        grid=(n // tm,),
        in_specs=[
            pl.BlockSpec((tm, Q_DIM), lambda i: (i, 0)),
            pl.BlockSpec((tm, LRU_WIDTH), lambda i: (i, 0)),
            pl.BlockSpec((tm, d), lambda i: (i, 0)),
            _const_spec((Q_DIM, d)), _const_spec((LRU_WIDTH, d)), _const_spec((1, d)),
            _const_spec((d, dq)),
        ],
        out_specs=[
            pl.BlockSpec((tm, d), lambda i: (i, 0)),
            pl.BlockSpec((tm, d), lambda i: (i, 0)),
            pl.BlockSpec((PEER_HEADS, tm, dq // PEER_HEADS), lambda i: (0, i, 0)),
        ],
        out_shape=[
            jax.ShapeDtypeStruct((n, d), F32),
            jax.ShapeDtypeStruct((n, d), F32),
            jax.ShapeDtypeStruct((PEER_HEADS, n, dq // PEER_HEADS), BF16),
        ],
        compiler_params=_cparams(("parallel",)),
        name="out_proj_peer_query",
    )(attn_n, lru_n, x2, wa, wl, g, wq)


def _top16_rows(s, ids):
    vals, picks = [], []
    for _ in range(PEER_TOPK):
        m = jnp.max(s, axis=0, keepdims=True)
        pick = jnp.min(jnp.where(s == m, ids, jnp.inf), axis=0, keepdims=True)
        s = jnp.where(ids == pick, -jnp.inf, s)
        vals.append(m)
        picks.append(pick)
    return jnp.concatenate(vals, axis=0), jnp.concatenate(picks, axis=0)


def _peer_select(qh, keys):
    tm = qh.shape[0]
    nt = (((1,), (1,)), ((), ()))
    key_ids = lax.broadcasted_iota(I32, (N_KEYS, tm), 0).astype(F32)
    tops = []
    for p in range(2):
        qp = qh[:, p * N_KEYS:(p + 1) * N_KEYS]
        s = lax.dot_general(keys[p], qp, nt, preferred_element_type=F32)
        tops.append(_top16_rows(s, key_ids))
    (v0, i0), (v1, i1) = tops
    sub8 = lax.broadcasted_iota(I32, (8, tm), 0).astype(F32)
    cand = [v0[0:1] + v1]
    cid = [lax.broadcasted_iota(I32, (PEER_TOPK, tm), 0).astype(F32)]
    eid = [i0[0:1] * N_KEYS + i1]
    for k1 in range(1, PEER_TOPK):
        cand.append(v0[k1:k1 + 1] + v1[0:8])
        cid.append(sub8 + float(k1 * PEER_TOPK))
        eid.append(i0[k1:k1 + 1] * N_KEYS + i1[0:8])
    cand = jnp.concatenate(cand, axis=0)
    cid = jnp.concatenate(cid, axis=0)
    eid = jnp.concatenate(eid, axis=0)
    best, experts = [], []
    for _ in range(PEER_TOPK):
        m = jnp.max(cand, axis=0, keepdims=True)
        pick = jnp.min(jnp.where(cand == m, cid, jnp.inf), axis=0, keepdims=True)
        sel = cid == pick
        experts.append(jnp.max(jnp.where(sel, eid, -1.0), axis=0, keepdims=True))
        cand = jnp.where(sel, -jnp.inf, cand)
        best.append(m)
    best = jnp.concatenate(best, axis=0)
    ex = jnp.exp(best - best[0:1])
    return jnp.concatenate(experts, axis=0), ex / jnp.sum(ex, axis=0, keepdims=True)


def _expert_table(u, v):
    ne, d = u.shape
    return jnp.concatenate([u.astype(BF16), v.astype(BF16)], axis=1).reshape(ne, 2 * d // 128, 128)


SUBLANES = 8
LANES = 128


def _fold_pair(x, y, sh, rows):
    x2 = x + pltpu.roll(x, sh, 0)
    y2 = y + pltpu.roll(y, SUBLANES - sh, 0)
    return jnp.where((rows // sh) % 2 == 1, x2, y2)


def _fold_order():
    rows = np.arange(SUBLANES)[:, None]

    def fold(x, y, sh):
        x2 = x + np.roll(x, sh, 0)
        y2 = y + np.roll(y, SUBLANES - sh, 0)
        return np.where((rows // sh) % 2 == 1, x2, y2)

    vals = []
    for i in range(SUBLANES):
        v = np.zeros((SUBLANES, SUBLANES * SUBLANES), np.int64)
        v[np.arange(SUBLANES), i * SUBLANES + np.arange(SUBLANES)] = 1
        vals.append(v)
    sh = SUBLANES // 2
    while len(vals) > 1:
        vals = [fold(vals[2 * k], vals[2 * k + 1], sh) for k in range(len(vals) // 2)]
        sh //= 2
    out = vals[0].reshape(SUBLANES, SUBLANES, SUBLANES)
    src_of_row = [int(np.argmax(out[r].sum(-1))) for r in range(SUBLANES)]
    for r in range(SUBLANES):
        assert (out[r, src_of_row[r]] == 1).all() and out[r].sum() == SUBLANES
    order = [0] * SUBLANES
    for r, p in enumerate(src_of_row):
        order[p] = r
    return tuple(order)


_FOLD_ORDER = _fold_order()


def _sublane_sums(parts, rows):
    vals = [parts[r] for r in _FOLD_ORDER]
    sh = SUBLANES // 2
    while len(vals) > 1:
        vals = [_fold_pair(vals[2 * k], vals[2 * k + 1], sh, rows) for k in range(len(vals) // 2)]
        sh //= 2
    return vals[0]


TOKEN_BLOCK = 128


def _expert_kernel(tb, q0_ref, qn_ref, keys_ref, tab_ref, xw_ref, h_ref, gf_ref,
                   y_ref, buf_a_ref, buf_b_ref, sem_ref, wrep_ref, e_ref, g_ref, idxv_ref, idx_ref,
                   isem_ref):
    i = pl.program_id(0)
    nsteps = pl.num_programs(0)
    step_tokens = 2 * tb
    steps_per_blk = TOKEN_BLOCK // step_tokens
    assert steps_per_blk == PEER_HEADS
    blk = i // steps_per_blk
    sub = i % steps_per_blk
    nblk = nsteps // steps_per_blk
    rows_per_half = tb * PEER_ROWS
    ds = xw_ref.shape[1] // LANES
    bufs = (buf_a_ref, buf_b_ref)

    def as_tile(ref, r):
        return jnp.concatenate([ref[r:r + 1, s * LANES:(s + 1) * LANES] for s in range(ds)], axis=0)

    tail = TOKEN_BLOCK - step_tokens

    def row_copy(e, half, r):
        return pltpu.make_async_copy(tab_ref.at[e], bufs[half].at[r], sem_ref.at[half])

    def half_wait(half):
        pltpu.make_async_copy(tab_ref.at[pl.ds(0, rows_per_half)], bufs[half], sem_ref.at[half]).wait()

    def select_head(q_ref, hd, slot):
        experts, gates = _peer_select(q_ref[hd], keys_ref[hd])
        r0 = pl.multiple_of(hd * PEER_TOPK, PEER_TOPK)
        e_ref[pl.ds(r0, PEER_TOPK), :] = experts
        g_ref[slot, pl.ds(r0, PEER_TOPK), :] = gates

    def ids_copy(lo, hi):
        return pltpu.make_async_copy(idxv_ref.at[pl.ds(lo, hi - lo)], idx_ref.at[pl.ds(lo, hi - lo)], isem_ref)

    @pl.when(i == 0)
    def _():
        def head_body(hd, carry):
            select_head(q0_ref, hd, 0)
            return carry
        lax.fori_loop(0, PEER_HEADS, head_body, 0)
        idxv_ref[...] = jnp.transpose(e_ref[...]).astype(I32)
        ids_copy(0, TOKEN_BLOCK).start()
        ids_copy(0, TOKEN_BLOCK).wait()

        def body(t, carry):
            for j in range(PEER_ROWS):
                row_copy(idx_ref[t, j], 0, t * PEER_ROWS + j).start()
            return carry
        lax.fori_loop(0, tb, body, 0)

    late_ids = jnp.logical_and(sub == 0, blk > 0)
    early_ids = jnp.logical_and(sub == steps_per_blk - 1, blk + 1 < nblk)

    @pl.when(late_ids)
    def _():
        ids_copy(tail, TOKEN_BLOCK).start()


    rows8 = lax.broadcasted_iota(I32, (SUBLANES, LANES), 0)
    lane = lax.broadcasted_iota(I32, (PEER_ROWS, TOKEN_BLOCK), 1)
    half_rows = PEER_ROWS // 2
    gslot = blk % 2
    gain = as_tile(gf_ref, 0)

    def process(half, next_row0, ride_along=None):
        other = 1 - half
        t0 = half * tb
        half_wait(half)
        if ride_along is not None:
            ride_along()
        buf = bufs[half]
        base = sub * step_tokens + t0

        def prefetch(t, j):
            row_copy(idx_ref[next_row0 + t, j], other, t * PEER_ROWS + j).start(priority=j % 2)

        pre = jnp.zeros((PEER_ROWS, TOKEN_BLOCK), F32)
        for t in range(tb):
            x_t = as_tile(xw_ref, t0 + t).astype(BF16)
            sums = []
            for g0 in range(0, PEER_ROWS, SUBLANES):
                for j in range(g0 // 2, (g0 + SUBLANES) // 2):
                    prefetch(t, j)
                parts = []
                for j in range(g0, g0 + SUBLANES):
                    p = (buf[t * PEER_ROWS + j, 0:ds, :] * x_t).astype(F32)
                    parts.append(p[0:SUBLANES] + p[SUBLANES:2 * SUBLANES])
                sums.append(_sublane_sums(parts, rows8))
            rowsum = jnp.sum(jnp.concatenate(sums, axis=0), axis=-1, keepdims=True)
            pre = jnp.where(lane == base + t, rowsum, pre)
        w = g_ref[gslot] * _gelu(pre)
        for t in range(tb):
            wcol = jnp.sum(jnp.where(lane == base + t, w, 0.0), axis=-1, keepdims=True)
            wrep_ref[t] = jnp.broadcast_to(wcol, (PEER_ROWS, LANES))
        n_acc = 4
        for t in range(tb):
            accs = [jnp.zeros((ds, LANES), F32) for _ in range(n_acc)]
            for j in range(PEER_ROWS):
                if j % 2 == 0:
                    prefetch(t, half_rows + j // 2)
                v = buf[t * PEER_ROWS + j, ds:2 * ds, :].astype(F32)
                accs[j % n_acc] = accs[j % n_acc] + v * wrep_ref[t, j:j + 1, :]
            hp = as_tile(h_ref, t0 + t) + ((accs[0] + accs[1]) + (accs[2] + accs[3]))
            ms = jnp.mean(jnp.mean(hp * hp, axis=-1, keepdims=True), axis=0, keepdims=True)
            y_t = hp * lax.rsqrt(ms + EPS) * gain
            for s in range(ds):
                y_ref[t0 + t:t0 + t + 1, s * LANES:(s + 1) * LANES] = y_t[s:s + 1, :]

    process(0, sub * step_tokens + tb, ride_along=lambda: select_head(qn_ref, sub, (blk + 1) % 2))

    @pl.when(early_ids)
    def _():
        idxv_ref[...] = jnp.transpose(e_ref[...]).astype(I32)
        ids_copy(0, tail).start()
        ids_copy(0, tail).wait()

    process(1, ((sub + 1) % steps_per_blk) * step_tokens)

    @pl.when(late_ids)
    def _():
        ids_copy(tail, TOKEN_BLOCK).wait()

    @pl.when(i == nsteps - 1)
    def _():
        half_wait(0)


def _experts(q3, keys_bf, tab3, xw, h, gf, tb):
    n, d = h.shape
    ds, lanes = d // LANES, LANES
    assert n % TOKEN_BLOCK == 0
    step_tokens = 2 * tb
    nsteps = n // step_tokens
    steps_per_blk = TOKEN_BLOCK // step_tokens
    nblk = n // TOKEN_BLOCK
    dk = q3.shape[2]
    buf = pltpu.VMEM((tb * PEER_ROWS, 2 * ds, lanes), BF16)
    return pl.pallas_call(
        functools.partial(_expert_kernel, tb),
        grid=(nsteps,),
        in_specs=[
            pl.BlockSpec((PEER_HEADS, TOKEN_BLOCK, dk), lambda i: (0, 0, 0)),
            pl.BlockSpec((PEER_HEADS, TOKEN_BLOCK, dk),
                         lambda i: (0, jnp.minimum(i // steps_per_blk + 1, nblk - 1), 0)),
            _const_spec((PEER_HEADS, 2, N_KEYS, N_KEYS)),
            pl.BlockSpec(memory_space=pl.ANY),
            pl.BlockSpec((step_tokens, d), lambda i: (i, 0)),
            pl.BlockSpec((step_tokens, d), lambda i: (i, 0)),
            _const_spec((1, d)),
        ],
        out_specs=pl.BlockSpec((step_tokens, d), lambda i: (i, 0)),
        out_shape=jax.ShapeDtypeStruct((n, d), F32),
        scratch_shapes=[
            buf, buf,
            pltpu.SemaphoreType.DMA((2,)),
            pltpu.VMEM((tb, PEER_ROWS, LANES), F32),
            pltpu.VMEM((PEER_ROWS, TOKEN_BLOCK), F32),
            pltpu.VMEM((2, PEER_ROWS, TOKEN_BLOCK), F32),
            pltpu.VMEM((TOKEN_BLOCK, PEER_ROWS), I32),
            pltpu.SMEM((TOKEN_BLOCK, PEER_ROWS), I32),
            pltpu.SemaphoreType.DMA(()),
        ],
        compiler_params=_cparams(("arbitrary",)),
        name="peer_experts",
    )(q3, q3, keys_bf, tab3, xw, h, gf)


def _row_tile(n, pref):
    t = pref
    while n % t:
        t //= 2
    return t


def kernel(x, meta_tokens, norm_mix_g, w_in, b_in, sinks, conv_w, conv_b, w_r, b_r, w_i, b_i, lru_lambda, gn_attn_g, gn_lru_g, w_out, norm_ffn_g, peer_wq, peer_sub_keys, peer_u, peer_v, final_norm_g):
    n_batch, seq, d = x.shape
    n = n_batch * seq
    assert w_in.shape[0] == 1 and seq % ATT_BLOCK == 0
    x2 = x.reshape(n, d)
    row = lambda a: a.reshape(1, -1)

    w_in_bf = w_in[0].astype(BF16)
    g_mix = row(norm_mix_g[0])
    b_in_r = row(b_in[0])
    qkv, xb, gate = _input_proj(x2, g_mix, w_in_bf, b_in_r, _row_tile(n, 256))
    qkv_m, xb_m, gate_m = _input_proj(meta_tokens.astype(F32), g_mix, w_in_bf, b_in_r, N_META)

    km = qkv_m[:, Q_DIM:Q_DIM + KV_DIM]
    vm = qkv_m[:, Q_DIM + KV_DIM:]
    attn_n = _attention(qkv, km, vm, sinks[0].astype(F32), row(gn_attn_g[0]), n_batch)

    lru_args = (conv_w[0], row(conv_b[0]), _block_diag_groups(w_r[0]).astype(BF16), row(b_r[0]),
                _block_diag_groups(w_i[0]).astype(BF16), row(b_i[0]), row(lru_lambda[0]),
                row(gn_lru_g[0]))
    zeros8 = jnp.zeros((8, LRU_WIDTH), F32)
    _, h_meta = _lru(xb_m, gate_m, *lru_args, zeros8, zeros8, 1, N_META)
    lru_n, _ = _lru(xb, gate, *lru_args, h_meta, xb_m[N_META - 8:], n_batch, _row_tile(seq, 256))

    w_out_bf = w_out[0].astype(BF16)
    h1, xw, q = _mix(attn_n, lru_n, x2, w_out_bf[:Q_DIM], w_out_bf[Q_DIM:], row(norm_ffn_g[0]),
                     peer_wq[0].astype(BF16), _row_tile(n, 256))

    tab3 = _expert_table(peer_u[0], peer_v[0])
    y = _experts(q, peer_sub_keys[0].astype(BF16), tab3, xw, h1, row(final_norm_g), 8)
    return y.reshape(n_batch, seq, d)
```

```python
import functools

import numpy as np
import jax
import jax.numpy as jnp
from jax import lax
from jax.experimental import pallas as pl
from jax.experimental.pallas import tpu as pltpu

F32 = jnp.float32
BF16 = jnp.bfloat16
I32 = jnp.int32

EPS = 1e-6
N_META = 16
HEAD_DIM = 64
N_Q_HEADS = 16
N_KV_HEADS = 2
GROUP = N_Q_HEADS // N_KV_HEADS
Q_DIM = N_Q_HEADS * HEAD_DIM
KV_DIM = N_KV_HEADS * HEAD_DIM
ATT_BLOCK = 128
LRU_WIDTH = 1024
LRU_BLOCKS = 16
LRU_BW = LRU_WIDTH // LRU_BLOCKS
LRU_C = 8.0
GATE_GROUP = 256
PEER_HEADS = 8
N_KEYS = 128
PEER_TOPK = 16
PEER_ROWS = PEER_HEADS * PEER_TOPK
NEG_BIG = -1e30

VMEM_LIMIT = 56 * 1024 * 1024


def _cparams(sem, vmem=VMEM_LIMIT):
    return pltpu.CompilerParams(dimension_semantics=sem, vmem_limit_bytes=vmem)


def _const_spec(shape):
    nd = len(shape)
    return pl.BlockSpec(shape, lambda *_: (0,) * nd, pipeline_mode=pl.Buffered(1))


def _rms(x, g):
    return x * lax.rsqrt(jnp.mean(x * x, axis=-1, keepdims=True) + EPS) * g


def _gelu(x):
    return jax.nn.gelu(x, approximate=True)


def _proj_kernel(x_ref, g_ref, w_ref, b_ref, qkv_ref, xb_ref, gate_ref):
    hn = _rms(x_ref[...], g_ref[...]).astype(BF16)
    n_qkv = Q_DIM + 2 * KV_DIM

    def proj(lo, hi):
        return jnp.dot(hn, w_ref[:, lo:hi], preferred_element_type=F32) + b_ref[:, lo:hi]

    step = 256
    for lo in range(0, n_qkv, step):
        qkv_ref[:, lo:lo + step] = proj(lo, lo + step).astype(BF16)
    for lo in range(0, LRU_WIDTH, step):
        xb_ref[:, lo:lo + step] = proj(n_qkv + lo, n_qkv + lo + step)
        gate_ref[:, lo:lo + step] = proj(n_qkv + LRU_WIDTH + lo, n_qkv + LRU_WIDTH + lo + step)


def _input_proj(x2, g, w_bf, b, tm):
    n, d = x2.shape
    n_qkv = Q_DIM + 2 * KV_DIM
    in_dim = w_bf.shape[1]
    return pl.pallas_call(
        _proj_kernel,
        grid=(n // tm,),
        in_specs=[
            pl.BlockSpec((tm, d), lambda i: (i, 0)),
            _const_spec((1, d)),
            _const_spec((d, in_dim)),
            _const_spec((1, in_dim)),
        ],
        out_specs=[
            pl.BlockSpec((tm, n_qkv), lambda i: (i, 0)),
            pl.BlockSpec((tm, LRU_WIDTH), lambda i: (i, 0)),
            pl.BlockSpec((tm, LRU_WIDTH), lambda i: (i, 0)),
        ],
        out_shape=[
            jax.ShapeDtypeStruct((n, n_qkv), BF16),
            jax.ShapeDtypeStruct((n, LRU_WIDTH), F32),
            jax.ShapeDtypeStruct((n, LRU_WIDTH), F32),
        ],
        compiler_params=_cparams(("parallel",)),
        name="input_proj",
    )(x2, g, w_bf, b)


def _attn_kernel(nb, q_ref, kc_ref, kp_ref, vc_ref, vp_ref, km_ref, vm_ref, sink_ref, g_ref,
                 o_ref, acc_ref):
    blk = pl.program_id(0) % nb
    row = lax.broadcasted_iota(I32, (ATT_BLOCK, 2 * ATT_BLOCK), 0)
    col = lax.broadcasted_iota(I32, (ATT_BLOCK, 2 * ATT_BLOCK), 1)
    no_prev = jnp.where(blk > 0, 0, 2 * ATT_BLOCK)
    in_prev = jnp.logical_and(col < ATT_BLOCK, col > row + no_prev)
    in_cur = jnp.logical_and(col >= ATT_BLOCK, (col - ATT_BLOCK) <= row)
    band = jnp.logical_or(in_prev, in_cur)
    scale = HEAD_DIM ** -0.5
    nt = (((1,), (1,)), ((), ()))
    kvs = []
    for g in range(N_KV_HEADS):
        ks = slice(g * HEAD_DIM, (g + 1) * HEAD_DIM)
        kvs.append((jnp.concatenate([kp_ref[:, ks], kc_ref[:, ks]], axis=0),
                    jnp.concatenate([vp_ref[:, ks], vc_ref[:, ks]], axis=0),
                    km_ref[:, ks], vm_ref[:, ks]))

    def scores(h):
        kb, _, kmg, _ = kvs[h // GROUP]
        qh = q_ref[:, h * HEAD_DIM:(h + 1) * HEAD_DIM]
        sb = lax.dot_general(qh, kb, nt, preferred_element_type=F32) * scale
        sm = lax.dot_general(qh, kmg, nt, preferred_element_type=F32) * scale
        return jnp.where(band, sb, NEG_BIG), sm

    nxt = scores(0)
    for h in range(N_Q_HEADS):
        sb, sm = nxt
        if h + 1 < N_Q_HEADS:
            nxt = scores(h + 1)
        _, vb, _, vmg = kvs[h // GROUP]
        sink = sink_ref[h]
        m = jnp.maximum(jnp.max(sb, axis=-1, keepdims=True), jnp.max(sm, axis=-1, keepdims=True))
        m = jnp.maximum(m, sink)
        pb = jnp.exp(sb - m)
        pm = jnp.exp(sm - m)
        den = (jnp.sum(pb, axis=-1, keepdims=True) + jnp.sum(pm, axis=-1, keepdims=True)
               + jnp.exp(sink - m))
        o = (jnp.dot(pb.astype(BF16), vb, preferred_element_type=F32)
             + jnp.dot(pm.astype(BF16), vmg, preferred_element_type=F32))
        acc_ref[:, h * HEAD_DIM:(h + 1) * HEAD_DIM] = o / den
    o_ref[...] = _rms(acc_ref[...], g_ref[...]).astype(BF16)


def _attention(qkv, km, vm, sinks, g, n_batch):
    n = qkv.shape[0]
    nblk = n // ATT_BLOCK
    nb = nblk // n_batch
    kcol = Q_DIM // KV_DIM
    return pl.pallas_call(
        functools.partial(_attn_kernel, nb),
        grid=(nblk,),
        in_specs=[
            pl.BlockSpec((ATT_BLOCK, Q_DIM), lambda i: (i, 0)),
            pl.BlockSpec((ATT_BLOCK, KV_DIM), lambda i: (i, kcol)),
            pl.BlockSpec((ATT_BLOCK, KV_DIM), lambda i: (jnp.maximum(i - 1, 0), kcol)),
            pl.BlockSpec((ATT_BLOCK, KV_DIM), lambda i: (i, kcol + 1)),
            pl.BlockSpec((ATT_BLOCK, KV_DIM), lambda i: (jnp.maximum(i - 1, 0), kcol + 1)),
            _const_spec((N_META, KV_DIM)),
            _const_spec((N_META, KV_DIM)),
            pl.BlockSpec(memory_space=pltpu.SMEM),
            _const_spec((1, Q_DIM)),
        ],
        out_specs=pl.BlockSpec((ATT_BLOCK, Q_DIM), lambda i: (i, 0)),
        out_shape=jax.ShapeDtypeStruct((n, Q_DIM), BF16),
        scratch_shapes=[pltpu.VMEM((ATT_BLOCK, Q_DIM), F32)],
        compiler_params=_cparams(("parallel",)),
        name="swa_attention",
    )(qkv, qkv, qkv, qkv, qkv, km, vm, sinks, g)


def _log_sigmoid(x):
    return -(jnp.maximum(-x, 0.0) + jnp.log1p(jnp.exp(-jnp.abs(x))))


def _lru_kernel(tt, x_ref, gate_ref, cw_ref, cb_ref, wr_ref, br_ref, wi_ref, bi_ref, lam_ref, g_ref,
                h0_ref, tail0_ref, o_ref, hlast_ref, xfull_ref, h_ref):
    t = pl.program_id(1)

    @pl.when(t == 0)
    def _():
        h_ref[...] = h0_ref[...]
        xfull_ref[0:8, :] = tail0_ref[...]

    x = x_ref[...]
    xfull_ref[8:8 + tt, :] = x
    xc = (cb_ref[...]
          + cw_ref[0:1, :] * xfull_ref[5:5 + tt, :]
          + cw_ref[1:2, :] * xfull_ref[6:6 + tt, :]
          + cw_ref[2:3, :] * xfull_ref[7:7 + tt, :]
          + cw_ref[3:4, :] * x)
    xfull_ref[0:8, :] = x[tt - 8:tt, :]

    xcb = xc.astype(BF16)
    r_parts, i_parts = [], []
    for gi in range(LRU_WIDTH // GATE_GROUP):
        sl = slice(gi * GATE_GROUP, (gi + 1) * GATE_GROUP)
        r_parts.append(jnp.dot(xcb[:, sl], wr_ref[gi], preferred_element_type=F32))
        i_parts.append(jnp.dot(xcb[:, sl], wi_ref[gi], preferred_element_type=F32))
    r = jax.nn.sigmoid(jnp.concatenate(r_parts, axis=-1) + br_ref[...])
    ig = jax.nn.sigmoid(jnp.concatenate(i_parts, axis=-1) + bi_ref[...])
    log_a = LRU_C * r * _log_sigmoid(lam_ref[...])
    a = jnp.exp(log_a)
    th = jnp.tanh(log_a)
    b = jnp.sqrt(-2.0 * th / (1.0 - th)) * (ig * xc)

    rows = lax.broadcasted_iota(I32, a.shape, 0)
    s = 1
    while s < tt:
        live = rows >= s
        a_sh = pltpu.roll(a, s, 0)
        b_sh = pltpu.roll(b, s, 0)
        b = jnp.where(live, a * b_sh + b, b)
        a = jnp.where(live, a * a_sh, a)
        s *= 2
    h = b + a * h_ref[0:1, :]
    hl = h[tt - 1:tt, :]
    h_ref[...] = jnp.broadcast_to(hl, h_ref.shape)
    hlast_ref[...] = jnp.broadcast_to(hl, hlast_ref.shape)

    o_ref[...] = _rms(_gelu(gate_ref[...]) * h, g_ref[...]).astype(BF16)


def _lru(xb, gate, cw, cb, wr_bd, br, wi_bd, bi, lam, g, h0, tail0, n_batch, tt):
    n = xb.shape[0]
    nt = n // n_batch // tt
    w = LRU_WIDTH
    ng = w // GATE_GROUP
    row_spec = pl.BlockSpec((tt, w), lambda b, t: (b * nt + t, 0))
    return pl.pallas_call(
        functools.partial(_lru_kernel, tt),
        grid=(n_batch, nt),
        in_specs=[
            row_spec, row_spec,
            _const_spec((4, w)), _const_spec((1, w)),
            _const_spec((ng, GATE_GROUP, GATE_GROUP)), _const_spec((1, w)),
            _const_spec((ng, GATE_GROUP, GATE_GROUP)), _const_spec((1, w)),
            _const_spec((1, w)), _const_spec((1, w)),
            _const_spec((8, w)), _const_spec((8, w)),
        ],
        out_specs=[row_spec, pl.BlockSpec((8, w), lambda b, t: (b * nt + t, 0))],
        out_shape=[jax.ShapeDtypeStruct((n, w), BF16),
                   jax.ShapeDtypeStruct((n_batch * nt * 8, w), F32)],
        scratch_shapes=[pltpu.VMEM((tt + 8, w), F32), pltpu.VMEM((8, w), F32)],
        compiler_params=_cparams(("arbitrary", "arbitrary")),
        name="rg_lru",
    )(xb, gate, cw, cb, wr_bd, br, wi_bd, bi, lam, g, h0, tail0)


def _block_diag_groups(w):
    per = GATE_GROUP // LRU_BW
    w4 = w.reshape(LRU_BLOCKS // per, per, LRU_BW, LRU_BW)
    eye = jnp.eye(per, dtype=w.dtype)
    return jnp.einsum('gacd,ab->gacbd', w4, eye).reshape(LRU_BLOCKS // per, GATE_GROUP, GATE_GROUP)


def _mix_kernel(a_ref, l_ref, x_ref, wa_ref, wl_ref, g_ref, wq_ref, h_ref, xw_ref, q_ref):
    h = (x_ref[...]
         + jnp.dot(a_ref[...], wa_ref[...], preferred_element_type=F32)
         + jnp.dot(l_ref[...], wl_ref[...], preferred_element_type=F32))
    h_ref[...] = h
    hn = _rms(h, g_ref[...])
    xw_ref[...] = hn
    q = jnp.dot(hn.astype(BF16), wq_ref[...], preferred_element_type=F32).astype(BF16)
    dk = q_ref.shape[2]
    for hd in range(q_ref.shape[0]):
        q_ref[hd] = q[:, hd * dk:(hd + 1) * dk]


def _mix(attn_n, lru_n, x2, wa, wl, g, wq, tm):
    n, d = x2.shape
    dq = wq.shape[1]
    return pl.pallas_call(
        _mix_kernel,
        grid=(n // tm,),
        in_specs=[
            pl.BlockSpec((tm, Q_DIM), lambda i: (i, 0)),
            pl.BlockSpec((tm, LRU_WIDTH), lambda i: (i, 0)),
            pl.BlockSpec((tm, d), lambda i: (i, 0)),
            _const_spec((Q_DIM, d)), _const_spec((LRU_WIDTH, d)), _const_spec((1, d)),
            _const_spec((d, dq)),
        ],
        out_specs=[
            pl.BlockSpec((tm, d), lambda i: (i, 0)),
            pl.BlockSpec((tm, d), lambda i: (i, 0)),
            pl.BlockSpec((PEER_HEADS, tm, dq // PEER_HEADS), lambda i: (0, i, 0)),
        ],
        out_shape=[
            jax.ShapeDtypeStruct((n, d), F32),
            jax.ShapeDtypeStruct((n, d), F32),
            jax.ShapeDtypeStruct((PEER_HEADS, n, dq // PEER_HEADS), BF16),
        ],
        compiler_params=_cparams(("parallel",)),
        name="out_proj_peer_query",
    )(attn_n, lru_n, x2, wa, wl, g, wq)


def _top16_rows(s, ids):
    vals, picks = [], []
    for _ in range(PEER_TOPK):
        m = jnp.max(s, axis=0, keepdims=True)
        pick = jnp.min(jnp.where(s == m, ids, jnp.inf), axis=0, keepdims=True)
        s = jnp.where(ids == pick, -jnp.inf, s)
        vals.append(m)
        picks.append(pick)
    return jnp.concatenate(vals, axis=0), jnp.concatenate(picks, axis=0)


def _peer_select(qh, keys):
    tm = qh.shape[0]
    nt = (((1,), (1,)), ((), ()))
    key_ids = lax.broadcasted_iota(I32, (N_KEYS, tm), 0).astype(F32)
    tops = []
    for p in range(2):
        qp = qh[:, p * N_KEYS:(p + 1) * N_KEYS]
        s = lax.dot_general(keys[p], qp, nt, preferred_element_type=F32)
        tops.append(_top16_rows(s, key_ids))
    (v0, i0), (v1, i1) = tops
    sub8 = lax.broadcasted_iota(I32, (8, tm), 0).astype(F32)
    cand = [v0[0:1] + v1]
    cid = [lax.broadcasted_iota(I32, (PEER_TOPK, tm), 0).astype(F32)]
    eid = [i0[0:1] * N_KEYS + i1]
    for k1 in range(1, PEER_TOPK):
        cand.append(v0[k1:k1 + 1] + v1[0:8])
        cid.append(sub8 + float(k1 * PEER_TOPK))
        eid.append(i0[k1:k1 + 1] * N_KEYS + i1[0:8])
    cand = jnp.concatenate(cand, axis=0)
    cid = jnp.concatenate(cid, axis=0)
    eid = jnp.concatenate(eid, axis=0)
    best, experts = [], []
    for _ in range(PEER_TOPK):
        m = jnp.max(cand, axis=0, keepdims=True)
        pick = jnp.min(jnp.where(cand == m, cid, jnp.inf), axis=0, keepdims=True)
        sel = cid == pick
        experts.append(jnp.max(jnp.where(sel, eid, -1.0), axis=0, keepdims=True))
        cand = jnp.where(sel, -jnp.inf, cand)
        best.append(m)
    best = jnp.concatenate(best, axis=0)
    ex = jnp.exp(best - best[0:1])
    return jnp.concatenate(experts, axis=0), ex / jnp.sum(ex, axis=0, keepdims=True)


def _expert_table(u, v):
    ne, d = u.shape
    return jnp.concatenate([u.astype(BF16), v.astype(BF16)], axis=1).reshape(ne, 2 * d // 128, 128)


SUBLANES = 8
LANES = 128


def _fold_pair(x, y, sh, rows):
    x2 = x + pltpu.roll(x, sh, 0)
    y2 = y + pltpu.roll(y, SUBLANES - sh, 0)
    return jnp.where((rows // sh) % 2 == 1, x2, y2)


def _fold_order():
    rows = np.arange(SUBLANES)[:, None]

    def fold(x, y, sh):
        x2 = x + np.roll(x, sh, 0)
        y2 = y + np.roll(y, SUBLANES - sh, 0)
        return np.where((rows // sh) % 2 == 1, x2, y2)

    vals = []
    for i in range(SUBLANES):
        v = np.zeros((SUBLANES, SUBLANES * SUBLANES), np.int64)
        v[np.arange(SUBLANES), i * SUBLANES + np.arange(SUBLANES)] = 1
        vals.append(v)
    sh = SUBLANES // 2
    while len(vals) > 1:
        vals = [fold(vals[2 * k], vals[2 * k + 1], sh) for k in range(len(vals) // 2)]
        sh //= 2
    out = vals[0].reshape(SUBLANES, SUBLANES, SUBLANES)
    src_of_row = [int(np.argmax(out[r].sum(-1))) for r in range(SUBLANES)]
    for r in range(SUBLANES):
        assert (out[r, src_of_row[r]] == 1).all() and out[r].sum() == SUBLANES
    order = [0] * SUBLANES
    for r, p in enumerate(src_of_row):
        order[p] = r
    return tuple(order)


_FOLD_ORDER = _fold_order()


def _sublane_sums(parts, rows):
    vals = [parts[r] for r in _FOLD_ORDER]
    sh = SUBLANES // 2
    while len(vals) > 1:
        vals = [_fold_pair(vals[2 * k], vals[2 * k + 1], sh, rows) for k in range(len(vals) // 2)]
        sh //= 2
    return vals[0]


TOKEN_BLOCK = 128


N_ROW_BUFS = 3


def _expert_kernel(tb, q0_ref, qn_ref, keys_ref, tab_ref, xw_ref, h_ref, gf_ref,
                   y_ref, buf_ref, sem_ref, wrep_ref, e_ref, g_ref, idxv_ref, idx_ref,
                   isem_ref):
    i = pl.program_id(0)
    nsteps = pl.num_programs(0)
    step_tokens = 2 * tb
    steps_per_blk = TOKEN_BLOCK // step_tokens
    assert steps_per_blk == PEER_HEADS
    blk = i // steps_per_blk
    sub = i % steps_per_blk
    nblk = nsteps // steps_per_blk
    rows_per_half = tb * PEER_ROWS
    ds = xw_ref.shape[1] // LANES

    def as_tile(ref, r):
        return jnp.concatenate([ref[r:r + 1, s * LANES:(s + 1) * LANES] for s in range(ds)], axis=0)


    def row_copy(e, slot, r):
        return pltpu.make_async_copy(tab_ref.at[e], buf_ref.at[slot, r], sem_ref.at[slot])

    def buf_wait(slot):
        pltpu.make_async_copy(tab_ref.at[pl.ds(0, rows_per_half)], buf_ref.at[slot], sem_ref.at[slot]).wait()

    def select_head(q_ref, hd, slot):
        experts, gates = _peer_select(q_ref[hd], keys_ref[hd])
        r0 = pl.multiple_of(hd * PEER_TOPK, PEER_TOPK)
        e_ref[pl.ds(r0, PEER_TOPK), :] = experts
        g_ref[slot, pl.ds(r0, PEER_TOPK), :] = gates

    def ids_copy(lo, hi):
        return pltpu.make_async_copy(idxv_ref.at[pl.ds(lo, hi - lo)], idx_ref.at[pl.ds(lo, hi - lo)], isem_ref)

    @pl.when(i == 0)
    def _():
        def head_body(hd, carry):
            select_head(q0_ref, hd, 0)
            return carry
        lax.fori_loop(0, PEER_HEADS, head_body, 0)
        idxv_ref[...] = jnp.transpose(e_ref[...]).astype(I32)
        ids_copy(0, TOKEN_BLOCK).start()
        ids_copy(0, TOKEN_BLOCK).wait()

        def body(t, carry):
            for j in range(PEER_ROWS):
                row_copy(idx_ref[t, j], t // tb, (t % tb) * PEER_ROWS + j).start()
            return carry
        lax.fori_loop(0, step_tokens, body, 0)

    @pl.when(blk + 1 < nblk)
    def _():
        select_head(qn_ref, sub, (blk + 1) % 2)

    @pl.when(jnp.logical_and(sub == steps_per_blk - 1, blk + 1 < nblk))
    def _():
        idxv_ref[...] = jnp.transpose(e_ref[...]).astype(I32)
        ids_copy(0, TOKEN_BLOCK).start()
        ids_copy(0, TOKEN_BLOCK).wait()

    rows8 = lax.broadcasted_iota(I32, (SUBLANES, LANES), 0)
    lane = lax.broadcasted_iota(I32, (PEER_ROWS, TOKEN_BLOCK), 1)
    half_rows = PEER_ROWS // 2
    gslot = blk % 2
    gain = as_tile(gf_ref, 0)

    def process(half, next_row0):
        slot = (2 * i + half) % N_ROW_BUFS
        other = (2 * i + half + 2) % N_ROW_BUFS
        t0 = half * tb
        buf_wait(slot)
        buf = buf_ref.at[slot]
        base = sub * step_tokens + t0

        def prefetch(t, j):
            row_copy(idx_ref[next_row0 + t, j], other, t * PEER_ROWS + j).start(priority=j % 2)

        pre = jnp.zeros((PEER_ROWS, TOKEN_BLOCK), F32)
        for t in range(tb):
            x_t = as_tile(xw_ref, t0 + t).astype(BF16)
            sums = []
            for g0 in range(0, PEER_ROWS, SUBLANES):
                for j in range(g0 // 2, (g0 + SUBLANES) // 2):
                    prefetch(t, j)
                parts = []
                for j in range(g0, g0 + SUBLANES):
                    p = (buf[t * PEER_ROWS + j, 0:ds, :] * x_t).astype(F32)
                    parts.append(p[0:SUBLANES] + p[SUBLANES:2 * SUBLANES])
                sums.append(_sublane_sums(parts, rows8))
            rowsum = jnp.sum(jnp.concatenate(sums, axis=0), axis=-1, keepdims=True)
            pre = jnp.where(lane == base + t, rowsum, pre)
        w = g_ref[gslot] * _gelu(pre)
        for t in range(tb):
            wcol = jnp.sum(jnp.where(lane == base + t, w, 0.0), axis=-1, keepdims=True)
            wrep_ref[t] = jnp.broadcast_to(wcol, (PEER_ROWS, LANES))
        n_acc = 4
        for t in range(tb):
            accs = [jnp.zeros((ds, LANES), F32) for _ in range(n_acc)]
            for j in range(PEER_ROWS):
                if j % 2 == 0:
                    prefetch(t, half_rows + j // 2)
                v = buf[t * PEER_ROWS + j, ds:2 * ds, :].astype(F32)
                accs[j % n_acc] = accs[j % n_acc] + v * wrep_ref[t, j:j + 1, :]
            hp = as_tile(h_ref, t0 + t) + ((accs[0] + accs[1]) + (accs[2] + accs[3]))
            ms = jnp.mean(jnp.mean(hp * hp, axis=-1, keepdims=True), axis=0, keepdims=True)
            y_t = hp * lax.rsqrt(ms + EPS) * gain
            for s in range(ds):
                y_ref[t0 + t:t0 + t + 1, s * LANES:(s + 1) * LANES] = y_t[s:s + 1, :]

    nxt = ((sub + 1) % steps_per_blk) * step_tokens
    process(0, nxt)
    process(1, nxt + tb)

    @pl.when(i == nsteps - 1)
    def _():
        buf_wait((2 * nsteps) % N_ROW_BUFS)
        buf_wait((2 * nsteps + 1) % N_ROW_BUFS)


def _experts(q3, keys_bf, tab3, xw, h, gf, tb):
    n, d = h.shape
    ds, lanes = d // LANES, LANES
    assert n % TOKEN_BLOCK == 0
    step_tokens = 2 * tb
    nsteps = n // step_tokens
    steps_per_blk = TOKEN_BLOCK // step_tokens
    nblk = n // TOKEN_BLOCK
    dk = q3.shape[2]
    buf = pltpu.VMEM((N_ROW_BUFS, tb * PEER_ROWS, 2 * ds, lanes), BF16)
    return pl.pallas_call(
        functools.partial(_expert_kernel, tb),
        grid=(nsteps,),
        in_specs=[
            pl.BlockSpec((PEER_HEADS, TOKEN_BLOCK, dk), lambda i: (0, 0, 0)),
            pl.BlockSpec((PEER_HEADS, TOKEN_BLOCK, dk),
                         lambda i: (0, jnp.minimum(i // steps_per_blk + 1, nblk - 1), 0)),
            _const_spec((PEER_HEADS, 2, N_KEYS, N_KEYS)),
            pl.BlockSpec(memory_space=pl.ANY),
            pl.BlockSpec((step_tokens, d), lambda i: (i, 0)),
            pl.BlockSpec((step_tokens, d), lambda i: (i, 0)),
            _const_spec((1, d)),
        ],
        out_specs=pl.BlockSpec((step_tokens, d), lambda i: (i, 0)),
        out_shape=jax.ShapeDtypeStruct((n, d), F32),
        scratch_shapes=[
            buf,
            pltpu.SemaphoreType.DMA((N_ROW_BUFS,)),
            pltpu.VMEM((tb, PEER_ROWS, LANES), F32),
            pltpu.VMEM((PEER_ROWS, TOKEN_BLOCK), F32),
            pltpu.VMEM((2, PEER_ROWS, TOKEN_BLOCK), F32),
            pltpu.VMEM((TOKEN_BLOCK, PEER_ROWS), I32),
            pltpu.SMEM((TOKEN_BLOCK, PEER_ROWS), I32),
            pltpu.SemaphoreType.DMA(()),
        ],
        compiler_params=_cparams(("arbitrary",)),
        name="peer_experts",
    )(q3, q3, keys_bf, tab3, xw, h, gf)


def _row_tile(n, pref):
    t = pref
    while n % t:
        t //= 2
    return t


def kernel(x, meta_tokens, norm_mix_g, w_in, b_in, sinks, conv_w, conv_b, w_r, b_r, w_i, b_i, lru_lambda, gn_attn_g, gn_lru_g, w_out, norm_ffn_g, peer_wq, peer_sub_keys, peer_u, peer_v, final_norm_g):
    n_batch, seq, d = x.shape
    n = n_batch * seq
    assert w_in.shape[0] == 1 and seq % ATT_BLOCK == 0
    x2 = x.reshape(n, d)
    row = lambda a: a.reshape(1, -1)

    w_in_bf = w_in[0].astype(BF16)
    g_mix = row(norm_mix_g[0])
    b_in_r = row(b_in[0])
    qkv, xb, gate = _input_proj(x2, g_mix, w_in_bf, b_in_r, _row_tile(n, 256))
    qkv_m, xb_m, gate_m = _input_proj(meta_tokens.astype(F32), g_mix, w_in_bf, b_in_r, N_META)

    km = qkv_m[:, Q_DIM:Q_DIM + KV_DIM]
    vm = qkv_m[:, Q_DIM + KV_DIM:]
    attn_n = _attention(qkv, km, vm, sinks[0].astype(F32), row(gn_attn_g[0]), n_batch)

    lru_args = (conv_w[0], row(conv_b[0]), _block_diag_groups(w_r[0]).astype(BF16), row(b_r[0]),
                _block_diag_groups(w_i[0]).astype(BF16), row(b_i[0]), row(lru_lambda[0]),
                row(gn_lru_g[0]))
    zeros8 = jnp.zeros((8, LRU_WIDTH), F32)
    _, h_meta = _lru(xb_m, gate_m, *lru_args, zeros8, zeros8, 1, N_META)
    lru_n, _ = _lru(xb, gate, *lru_args, h_meta, xb_m[N_META - 8:], n_batch, _row_tile(seq, 256))

    w_out_bf = w_out[0].astype(BF16)
    h1, xw, q = _mix(attn_n, lru_n, x2, w_out_bf[:Q_DIM], w_out_bf[Q_DIM:], row(norm_ffn_g[0]),
                     peer_wq[0].astype(BF16), _row_tile(n, 256))

    tab3 = _expert_table(peer_u[0], peer_v[0])
    y = _experts(q, peer_sub_keys[0].astype(BF16), tab3, xw, h1, row(final_norm_g), 8)
    return y.reshape(n_batch, seq, d)
```
